```python
import math
import jax
import jax.numpy as jnp
from jax import lax
import numpy as np

D_MODEL = 2048
BATCH = 4
SEQ = 4096
DEPTH = 2

CHUNK = 64
NORM_EPS = 1e-6
MIX_WIDTH = D_MODEL
GROUP_WIDTH = MIX_WIDTH // 4

HG_HEAD_DIM = 128
HG_HEADS = GROUP_WIDTH // HG_HEAD_DIM

SSM_HEAD_DIM = 64
SSM_HEADS = GROUP_WIDTH // SSM_HEAD_DIM
SSM_STATE = 128
SSM_GROUPS = 2
SSM_CONV = 4
SSM_BC = SSM_GROUPS * SSM_STATE
SSM_CONV_DIM = GROUP_WIDTH + 2 * SSM_BC

RG_BLOCKS = 8
RG_BLOCK = GROUP_WIDTH // RG_BLOCKS
RG_CONV = 4
RG_C = 8.0

RW_HEAD_DIM = 64
RW_HEADS = GROUP_WIDTH // RW_HEAD_DIM
RW_DECAY_RANK = 96
RW_A_RANK = 96
RW_GATE_RANK = 256
RW_GN_EPS = 64e-5

HG_COLS = 4 * GROUP_WIDTH
SSM_COLS = GROUP_WIDTH + SSM_CONV_DIM + SSM_HEADS
RG_COLS = 2 * GROUP_WIDTH
RW_COLS = 3 * GROUP_WIDTH + RW_DECAY_RANK + RW_A_RANK + RW_GATE_RANK
IN_COLS = HG_COLS + SSM_COLS + RG_COLS + RW_COLS

N_GROUPS = 4
EXPERTS_PER_GROUP = 8
N_EXPERTS = N_GROUPS * EXPERTS_PER_GROUP
TOP_K = 2
D_EXPERT = D_MODEL // 2
MOE_BLOCK = 128

kernel_name = 'hybrid_stream_hgrn2_ssd_rglru_rwkv7_hmoe'


def rms_norm(x, w, eps=NORM_EPS):
    xf = x.astype(jnp.float32)
    y = xf * lax.rsqrt(jnp.mean(xf * xf, axis=-1, keepdims=True) + eps)
    return (y * w.astype(jnp.float32)).astype(x.dtype)


def group_rms_norm(y, w, groups, eps=NORM_EPS):
    shp = y.shape
    yg = y.reshape(shp[:-1] + (groups, shp[-1] // groups))
    yg = yg * lax.rsqrt(jnp.mean(yg * yg, axis=-1, keepdims=True) + eps)
    return yg.reshape(shp) * w


def causal_dwconv(x, w, b):
    k = w.shape[0]
    y = lax.conv_general_dilated(x, w[:, None, :].astype(x.dtype), window_strides=(1,), padding=[(k - 1, 0)], dimension_numbers=('NWC', 'WIO', 'NWC'), feature_group_count=x.shape[-1])
    return y + b


def to_chunk_heads(t, heads):
    bsz, seq, width = t.shape
    return t.reshape(bsz, seq // CHUNK, CHUNK, heads, width // heads).transpose(1, 0, 3, 2, 4)


def hgrn2_mixer(p, lower_bound, norm_w):
    bsz, seq, _ = p.shape
    q, f_raw, i, g = jnp.split(p, 4, axis=-1)
    q = jax.nn.silu(q)
    f = lower_bound + (1.0 - lower_bound) * jax.nn.sigmoid(f_raw)
    k = 1.0 - f
    qc = to_chunk_heads(q, HG_HEADS)
    kc = to_chunk_heads(k, HG_HEADS)
    vc = to_chunk_heads(i, HG_HEADS)
    bc = jnp.cumsum(to_chunk_heads(jnp.log(f), HG_HEADS), axis=-2)
    causal = jnp.tril(jnp.ones((CHUNK, CHUNK), dtype=bool))[:, :, None]

    def chunk_step(state, inp):
        qt, kt, vt, bt = inp
        rel = jnp.where(causal, bt[:, :, :, None, :] - bt[:, :, None, :, :], -jnp.inf)
        scores = jnp.einsum('bhtk,bhsk,bhtsk->bhts', qt, kt, jnp.exp(rel))
        out = jnp.einsum('bhts,bhsv->bhtv', scores, vt) + jnp.einsum('bhtk,bhkv->bhtv', qt * jnp.exp(bt), state)
        b_end = bt[:, :, -1:, :]
        state = state * jnp.exp(b_end[:, :, 0, :, None]) + jnp.einsum('bhsk,bhsv->bhkv', kt * jnp.exp(b_end - bt), vt)
        return state, out

    s0 = jnp.zeros((bsz, HG_HEADS, HG_HEAD_DIM, HG_HEAD_DIM), jnp.float32)
    _, o = lax.scan(chunk_step, s0, (qc, kc, vc, bc))
    o = o.transpose(1, 0, 3, 2, 4).reshape(bsz, seq, GROUP_WIDTH)
    return group_rms_norm(o, norm_w, HG_HEADS) * jax.nn.silu(g)


def mamba2_mixer(p, conv_w, conv_b, dt_bias, a_log, d_skip, norm_w):
    bsz, seq, _ = p.shape
    n = seq // CHUNK
    z, xbc, dt = jnp.split(p, [GROUP_WIDTH, GROUP_WIDTH + SSM_CONV_DIM], axis=-1)
    xbc = jax.nn.silu(causal_dwconv(xbc, conv_w, conv_b))
    xs, bm, cm = jnp.split(xbc, [GROUP_WIDTH, GROUP_WIDTH + SSM_BC], axis=-1)
    rep = SSM_HEADS // SSM_GROUPS
    xs = xs.reshape(bsz, n, CHUNK, SSM_HEADS, SSM_HEAD_DIM)
    bm = jnp.repeat(bm.reshape(bsz, n, CHUNK, SSM_GROUPS, SSM_STATE), rep, axis=3)
    cm = jnp.repeat(cm.reshape(bsz, n, CHUNK, SSM_GROUPS, SSM_STATE), rep, axis=3)
    dt = jax.nn.softplus(dt + dt_bias).reshape(bsz, n, CHUNK, SSM_HEADS)
    a_cum = jnp.cumsum((dt * -jnp.exp(a_log)).transpose(0, 3, 1, 2), axis=-1)
    xdt = xs * dt[..., None]
    causal = jnp.tril(jnp.ones((CHUNK, CHUNK), dtype=bool))
    decay_in = jnp.exp(jnp.where(causal, a_cum[..., :, None] - a_cum[..., None, :], -jnp.inf))
    scores = jnp.einsum('bclhn,bcshn->bhcls', cm, bm) * decay_in
    y_diag = jnp.einsum('bhcls,bcshp->bclhp', scores, xdt)
    states = jnp.einsum('bclhn,bhcl,bclhp->bchpn', bm, jnp.exp(a_cum[..., -1:] - a_cum), xdt)

    def chunk_step(h, inp):
        s, dec = inp
        return h * jnp.exp(dec)[..., None, None] + s, h

    h0 = jnp.zeros((bsz, SSM_HEADS, SSM_HEAD_DIM, SSM_STATE), jnp.float32)
    _, prev = lax.scan(chunk_step, h0, (states.transpose(1, 0, 2, 3, 4), a_cum[..., -1].transpose(2, 0, 1)))
    y_off = jnp.einsum('bclhn,bchpn,bhcl->bclhp', cm, prev.transpose(1, 0, 2, 3, 4), jnp.exp(a_cum))
    y = (y_diag + y_off + xs * d_skip[:, None]).reshape(bsz, seq, GROUP_WIDTH)
    return group_rms_norm(y * jax.nn.silu(z), norm_w, SSM_GROUPS)


def rglru_mixer(p, conv_w, conv_b, w_a, b_a, w_x, b_x, lam):
    bsz, seq, _ = p.shape
    gate_in, xb = jnp.split(p, 2, axis=-1)
    xb = causal_dwconv(xb, conv_w, conv_b)
    xh = xb.reshape(bsz, seq, RG_BLOCKS, RG_BLOCK)
    r = jax.nn.sigmoid(jnp.einsum('blhi,hij->blhj', xh, w_a).reshape(bsz, seq, GROUP_WIDTH) + b_a)
    i = jax.nn.sigmoid(jnp.einsum('blhi,hij->blhj', xh, w_x).reshape(bsz, seq, GROUP_WIDTH) + b_x)
    log_a = -RG_C * r * jax.nn.softplus(-lam)
    a = jnp.exp(log_a)
    u = jnp.sqrt(-jnp.expm1(2.0 * log_a)) * (i * xb)

    def combine(c1, c2):
        return c1[0] * c2[0], c2[0] * c1[1] + c2[1]

    _, h = lax.associative_scan(combine, (a, u), axis=1)
    return h * jax.nn.gelu(gate_in, approximate=True)


def rwkv7_mixer(p, mu, w0, w2, a0, a2, g2, k_k, k_a, r_k, ln_w, ln_b):
    bsz, seq, _ = p.shape
    gw = GROUP_WIDTH
    prev = jnp.pad(p, ((0, 0), (1, 0), (0, 0)))[:, :-1]
    p = p + (prev - p) * mu
    r, k, v, wd, ad, gd = jnp.split(p, [gw, 2 * gw, 3 * gw, 3 * gw + RW_DECAY_RANK, 3 * gw + RW_DECAY_RANK + RW_A_RANK], axis=-1)
    w_log = -jax.nn.softplus(-(w0 + jnp.tanh(wd) @ w2)) - 0.5
    decay = jnp.exp(-jnp.exp(w_log))
    a = jax.nn.sigmoid(a0 + ad @ a2)
    g = jax.nn.sigmoid(gd) @ g2

    def heads(t):
        return t.reshape(bsz, seq, RW_HEADS, RW_HEAD_DIM)

    kk = heads(k * k_k)
    kk = kk / jnp.maximum(jnp.sqrt(jnp.sum(kk * kk, axis=-1, keepdims=True)), 1e-12)
    k = k * (1.0 + (a - 1.0) * k_a)
    rh, kh, vh, wh, ah = heads(r), heads(k), heads(v), heads(decay), heads(a)

    def time_step(state, inp):
        r_t, w_t, k_t, v_t, kk_t, a_t = inp
        sa = jnp.einsum('bhvk,bhk->bhv', state, -kk_t)
        state = state * w_t[:, :, None, :] + sa[..., None] * (kk_t * a_t)[:, :, None, :] + v_t[..., None] * k_t[:, :, None, :]
        return state, jnp.einsum('bhvk,bhk->bhv', state, r_t)

    def seq_first(t):
        return t.transpose(1, 0, 2, 3)

    s0 = jnp.zeros((bsz, RW_HEADS, RW_HEAD_DIM, RW_HEAD_DIM), jnp.float32)
    _, y = lax.scan(time_step, s0, (seq_first(rh), seq_first(wh), seq_first(kh), seq_first(vh), seq_first(kk), seq_first(ah)))
    y = y.transpose(1, 0, 2, 3)
    mean = jnp.mean(y, axis=-1, keepdims=True)
    var = jnp.mean(jnp.square(y - mean), axis=-1, keepdims=True)
    y = ((y - mean) * lax.rsqrt(var + RW_GN_EPS)).reshape(bsz, seq, gw) * ln_w + ln_b
    bonus = jnp.sum(rh * kh * r_k.reshape(RW_HEADS, RW_HEAD_DIM), axis=-1, keepdims=True) * vh
    return (y + bonus.reshape(bsz, seq, gw)) * g


def hier_moe(x, wg, bg, we, be, w_gate, w_up, w_down):
    bsz, seq, d = x.shape
    n_tok = bsz * seq
    n_assign = n_tok * TOP_K
    xt = x.reshape(n_tok, d)
    xf = xt.astype(jnp.float32)
    g_prob = jax.nn.softmax(xf @ wg.astype(jnp.float32) + bg.astype(jnp.float32), axis=-1)
    g_top, g_idx = lax.top_k(g_prob, 1)
    e_logits = (xf @ we.astype(jnp.float32) + be.astype(jnp.float32)).reshape(n_tok, N_GROUPS, EXPERTS_PER_GROUP)
    e_logits = jnp.take_along_axis(e_logits, g_idx[:, :, None], axis=1)[:, 0]
    e_top, e_idx = lax.top_k(jax.nn.softmax(e_logits, axis=-1), TOP_K)
    gates = g_top * e_top / jnp.sum(e_top, axis=-1, keepdims=True)
    expert_id = (g_idx * EXPERTS_PER_GROUP + e_idx).reshape(-1)
    token_id = jnp.repeat(jnp.arange(n_tok, dtype=jnp.int32), TOP_K)
    gate_flat = gates.reshape(-1)
    order = jnp.argsort(expert_id)
    sorted_e = expert_id[order]
    counts = jnp.bincount(expert_id, length=N_EXPERTS)
    padded = (counts + MOE_BLOCK - 1) // MOE_BLOCK * MOE_BLOCK
    pad_end = jnp.cumsum(padded)
    pad_start = pad_end - padded
    start = jnp.cumsum(counts) - counts
    dest = pad_start[sorted_e] + jnp.arange(n_assign, dtype=jnp.int32) - start[sorted_e]
    n_blocks = (n_assign + N_EXPERTS * (MOE_BLOCK - 1) + MOE_BLOCK - 1) // MOE_BLOCK
    n_slots = n_blocks * MOE_BLOCK
    slot_tok = jnp.full((n_slots,), n_tok, jnp.int32).at[dest].set(token_id[order])
    slot_gate = jnp.zeros((n_slots,), jnp.float32).at[dest].set(gate_flat[order])
    block_expert = jnp.minimum(jnp.searchsorted(pad_end, jnp.arange(n_blocks, dtype=jnp.int32) * MOE_BLOCK, side='right'), N_EXPERTS - 1)
    x_pad = jnp.concatenate([xt, jnp.zeros((1, d), xt.dtype)], axis=0)
    xb = x_pad[slot_tok].reshape(n_blocks, MOE_BLOCK, d)

    def expert_block(args):
        xblk, e = args
        hid = jax.nn.silu(xblk @ w_gate[e]) * (xblk @ w_up[e])
        return hid @ w_down[e]

    yb = lax.map(expert_block, (xb, block_expert)).reshape(n_slots, d)
    y = jnp.zeros((n_tok + 1, d), jnp.float32).at[slot_tok].add(yb.astype(jnp.float32) * slot_gate[:, None])
    return y[:n_tok].reshape(bsz, seq, d).astype(x.dtype)


def setup_inputs(seed: int = 0) -> dict:
    key = jax.random.key(seed)
    ks = iter(jax.random.split(key, 48))
    f32 = jnp.float32

    def nrm(shape, scale):
        return jax.random.normal(next(ks), shape, f32) * scale

    def unif(shape, lo, hi):
        return jax.random.uniform(next(ks), shape, f32, lo, hi)

    nl, gw = DEPTH, GROUP_WIDTH
    x = nrm((BATCH, SEQ, D_MODEL), 1.0)
    w_in = nrm((nl, D_MODEL, IN_COLS), D_MODEL ** -0.5)
    w_out = nrm((nl, MIX_WIDTH, D_MODEL), MIX_WIDTH ** -0.5)
    norm_mix_w = 1.0 + nrm((nl, D_MODEL), 0.02)
    norm_ffn_w = 1.0 + nrm((nl, D_MODEL), 0.02)
    final_norm_w = 1.0 + nrm((D_MODEL,), 0.02)
    hg_lb_param = 1.0 + nrm((nl, gw), 0.1)
    hg_norm_w = 1.0 + nrm((nl, gw), 0.02)
    ssm_conv_w = nrm((nl, SSM_CONV, SSM_CONV_DIM), SSM_CONV ** -0.5)
    ssm_conv_b = nrm((nl, SSM_CONV_DIM), 0.02)
    dt0 = jnp.exp(unif((nl, SSM_HEADS), math.log(1e-3), math.log(1e-1)))
    ssm_dt_bias = dt0 + jnp.log(-jnp.expm1(-dt0))
    ssm_a_log = jnp.log(unif((nl, SSM_HEADS), 1.0, 16.0))
    ssm_d = 1.0 + nrm((nl, SSM_HEADS), 0.1)
    ssm_norm_w = 1.0 + nrm((nl, gw), 0.02)
    rg_conv_w = nrm((nl, RG_CONV, gw), RG_CONV ** -0.5)
    rg_conv_b = nrm((nl, gw), 0.02)
    rg_w_a = nrm((nl, RG_BLOCKS, RG_BLOCK, RG_BLOCK), RG_BLOCK ** -0.5)
    rg_b_a = nrm((nl, gw), 0.02)
    rg_w_x = nrm((nl, RG_BLOCKS, RG_BLOCK, RG_BLOCK), RG_BLOCK ** -0.5)
    rg_b_x = nrm((nl, gw), 0.02)
    a_c = unif((nl, gw), 0.9, 0.999)
    s = a_c ** (1.0 / RG_C)
    rg_lambda = jnp.log(s) - jnp.log1p(-s)
    rw_mu = unif((nl, RW_COLS), 0.0, 1.0)
    rw_w0 = jnp.linspace(-7.0, -2.0, gw, dtype=f32)[None, :] + 0.5 + nrm((nl, gw), 0.1)
    rw_w2 = nrm((nl, RW_DECAY_RANK, gw), 0.5 * RW_DECAY_RANK ** -0.5)
    rw_a0 = nrm((nl, gw), 0.1)
    rw_a2 = nrm((nl, RW_A_RANK, gw), RW_A_RANK ** -0.5)
    rw_g2 = nrm((nl, RW_GATE_RANK, gw), RW_GATE_RANK ** -0.5)
    rw_k_k = 0.85 + nrm((nl, gw), 0.05)
    rw_k_a = 1.0 + nrm((nl, gw), 0.05)
    rw_r_k = nrm((nl, gw), 0.1)
    rw_ln_w = 1.0 + nrm((nl, gw), 0.02)
    rw_ln_b = nrm((nl, gw), 0.02)
    router_group_w = nrm((nl, D_MODEL, N_GROUPS), D_MODEL ** -0.5)
    router_group_b = nrm((nl, N_GROUPS), 0.01)
    router_expert_w = nrm((nl, D_MODEL, N_EXPERTS), D_MODEL ** -0.5)
    router_expert_b = nrm((nl, N_EXPERTS), 0.01)
    moe_w_gate = nrm((nl, N_EXPERTS, D_MODEL, D_EXPERT), D_MODEL ** -0.5)
    moe_w_up = nrm((nl, N_EXPERTS, D_MODEL, D_EXPERT), D_MODEL ** -0.5)
    moe_w_down = nrm((nl, N_EXPERTS, D_EXPERT, D_MODEL), D_EXPERT ** -0.5)
    return {'x': x, 'w_in': w_in, 'w_out': w_out, 'norm_mix_w': norm_mix_w, 'norm_ffn_w': norm_ffn_w, 'final_norm_w': final_norm_w, 'hg_lb_param': hg_lb_param, 'hg_norm_w': hg_norm_w, 'ssm_conv_w': ssm_conv_w, 'ssm_conv_b': ssm_conv_b, 'ssm_dt_bias': ssm_dt_bias, 'ssm_a_log': ssm_a_log, 'ssm_d': ssm_d, 'ssm_norm_w': ssm_norm_w, 'rg_conv_w': rg_conv_w, 'rg_conv_b': rg_conv_b, 'rg_w_a': rg_w_a, 'rg_b_a': rg_b_a, 'rg_w_x': rg_w_x, 'rg_b_x': rg_b_x, 'rg_lambda': rg_lambda, 'rw_mu': rw_mu, 'rw_w0': rw_w0, 'rw_w2': rw_w2, 'rw_a0': rw_a0, 'rw_a2': rw_a2, 'rw_g2': rw_g2, 'rw_k_k': rw_k_k, 'rw_k_a': rw_k_a, 'rw_r_k': rw_r_k, 'rw_ln_w': rw_ln_w, 'rw_ln_b': rw_ln_b, 'router_group_w': router_group_w, 'router_group_b': router_group_b, 'router_expert_w': router_expert_w, 'router_expert_b': router_expert_b, 'moe_w_gate': moe_w_gate, 'moe_w_up': moe_w_up, 'moe_w_down': moe_w_down}


def reference(x, w_in, w_out, norm_mix_w, norm_ffn_w, final_norm_w, hg_lb_param, hg_norm_w, ssm_conv_w, ssm_conv_b, ssm_dt_bias, ssm_a_log, ssm_d, ssm_norm_w, rg_conv_w, rg_conv_b, rg_w_a, rg_b_a, rg_w_x, rg_b_x, rg_lambda, rw_mu, rw_w0, rw_w2, rw_a0, rw_a2, rw_g2, rw_k_k, rw_k_a, rw_r_k, rw_ln_w, rw_ln_b, router_group_w, router_group_b, router_expert_w, router_expert_b, moe_w_gate, moe_w_up, moe_w_down):
    sm = jax.nn.softmax(hg_lb_param.astype(jnp.float32), axis=0)
    lower_bounds = jnp.cumsum(sm, axis=0) - sm[0]
    splits = [HG_COLS, HG_COLS + SSM_COLS, HG_COLS + SSM_COLS + RG_COLS]
    h = x
    for l in range(DEPTH):
        u = rms_norm(h, norm_mix_w[l])
        proj = (u @ w_in[l]).astype(jnp.float32)
        pa, pb, pc, pd = jnp.split(proj, splits, axis=-1)
        ya = hgrn2_mixer(pa, lower_bounds[l], hg_norm_w[l])
        yb = mamba2_mixer(pb, ssm_conv_w[l], ssm_conv_b[l], ssm_dt_bias[l], ssm_a_log[l], ssm_d[l], ssm_norm_w[l])
        yc = rglru_mixer(pc, rg_conv_w[l], rg_conv_b[l], rg_w_a[l], rg_b_a[l], rg_w_x[l], rg_b_x[l], rg_lambda[l])
        yd = rwkv7_mixer(pd, rw_mu[l], rw_w0[l], rw_w2[l], rw_a0[l], rw_a2[l], rw_g2[l], rw_k_k[l], rw_k_a[l], rw_r_k[l], rw_ln_w[l], rw_ln_b[l])
        mixed = jnp.concatenate([ya, yb, yc, yd], axis=-1).astype(h.dtype)
        h = h + mixed @ w_out[l]
        h = h + hier_moe(rms_norm(h, norm_ffn_w[l]), router_group_w[l], router_group_b[l], router_expert_w[l], router_expert_b[l], moe_w_gate[l], moe_w_up[l], moe_w_down[l])
    return rms_norm(h, final_norm_w)
```

```python
import functools
import math

import jax
import jax.numpy as jnp
from jax import lax
from jax.experimental import pallas as pl
from jax.experimental.pallas import tpu as pltpu

F32 = jnp.float32
BF16 = jnp.bfloat16

D_MODEL = 2048
GW = 512
NORM_EPS = 1e-6
HG_HEAD = 128
SSM_HEADS = 8
SSM_HEAD_DIM = 64
SSM_STATE = 128
SSM_GROUPS = 2
RG_C = 8.0
RW_HEADS = 8
RW_HEAD_DIM = 64
RW_LORA = 96
RW_GATE = 256
RW_GN_EPS = 64e-5
N_GROUPS = 4
E_PER_GROUP = 8
N_EXPERTS = 32
D_EXPERT = 1024

LANES = 128
SUBLANES = 8
VMEM_LIMIT = 56 * 1024 * 1024

C_HG_Q, C_HG_F, C_HG_I, C_HG_G = 0, 512, 1024, 1536
C_SSM_Z, C_SSM_X, C_SSM_B, C_SSM_C = 2048, 2560, 3072, 3328
C_RG_GATE, C_RG_X = 3584, 4096
C_RW_R, C_RW_K, C_RW_V = 4608, 5120, 5632
C_RW_WD, C_RW_AD, C_RW_GD = 6144, 6272, 6400
C_SSM_DT = 6656
IN_COLS_PAD = 6912

TIME_BLOCK = 256


def _cparams(sem):
    return pltpu.CompilerParams(dimension_semantics=sem, vmem_limit_bytes=VMEM_LIMIT)


def _split3(x):
    hi = x.astype(BF16)
    r1 = x - hi.astype(F32)
    mid = r1.astype(BF16)
    lo = (r1 - mid.astype(F32)).astype(BF16)
    return hi, mid, lo


def _dot_exact_rhs(x, m_bf16, dims=(((1,), (0,)), ((), ()))):
    hi, mid, lo = _split3(x)
    f = lambda a: lax.dot_general(a, m_bf16, dims, preferred_element_type=F32)
    return f(hi) + f(mid) + f(lo)


def _dot_exact_lhs(m_bf16, x, dims=(((1,), (0,)), ((), ()))):
    hi, mid, lo = _split3(x)
    f = lambda a: lax.dot_general(m_bf16, a, dims, preferred_element_type=F32)
    return f(hi) + f(mid) + f(lo)


def _dot3(a, b, dims=(((1,), (0,)), ((), ()))):
    ah = a.astype(BF16)
    al = (a - ah.astype(F32)).astype(BF16)
    bh = b.astype(BF16)
    bl = (b - bh.astype(F32)).astype(BF16)
    f = lambda p, q: lax.dot_general(p, q, dims, preferred_element_type=F32)
    return f(ah, bh) + f(ah, bl) + f(al, bh)


def _dotb(a, b, dims=(((1,), (0,)), ((), ()))):
    return lax.dot_general(a.astype(BF16), b.astype(BF16), dims, preferred_element_type=F32)


NT = (((1,), (1,)), ((), ()))
TN = (((0,), (0,)), ((), ()))


def _sigmoid(x):
    return 1.0 / (1.0 + jnp.exp(-x))


def _silu(x):
    return x * _sigmoid(x)


def _softplus(x):
    return jnp.maximum(x, 0.0) + jnp.log1p(jnp.exp(-jnp.abs(x)))


def _expm1(x):
    u = jnp.exp(x)
    return jnp.where(u == 1.0, x, jnp.where(u == 0.0, -1.0, (u - 1.0) * x / jnp.log(u)))


def _iota2(shape, axis):
    return lax.broadcasted_iota(jnp.int32, shape, axis)


def _inproj_body(x_ref, nw_ref, w_ref, o_ref, u_ref):
    @pl.when(pl.program_id(1) == 0)
    def _():
        x = x_ref[...]
        ms = jnp.mean(x * x, axis=-1, keepdims=True)
        u_ref[...] = (x * lax.rsqrt(ms + NORM_EPS) * nw_ref[...]).astype(BF16)

    o_ref[...] = jnp.dot(u_ref[...], w_ref[...], preferred_element_type=F32)


def _inproj(h, norm_w, w_pad, tm=1024, tn=768):
    n = h.shape[0]
    return pl.pallas_call(
        _inproj_body,
        grid=(n // tm, IN_COLS_PAD // tn),
        in_specs=[
            pl.BlockSpec((tm, D_MODEL), lambda i, j: (i, 0)),
            pl.BlockSpec((1, D_MODEL), lambda i, j: (0, 0)),
            pl.BlockSpec((D_MODEL, tn), lambda i, j: (0, j)),
        ],
        out_specs=pl.BlockSpec((tm, tn), lambda i, j: (i, j)),
        out_shape=jax.ShapeDtypeStruct((n, IN_COLS_PAD), F32),
        scratch_shapes=[pltpu.VMEM((tm, D_MODEL), BF16)],
        compiler_params=_cparams(("arbitrary", "arbitrary")),
        name="inproj",
    )(h, norm_w.reshape(1, D_MODEL), w_pad)


def _relayout_w_in(w):
    z = lambda c: jnp.zeros((w.shape[0], c), w.dtype)
    parts = [
        w[:, :3584],
        w[:, 3592:4616],
        w[:, 4616:6152],
        w[:, 6152:6248], z(32),
        w[:, 6248:6344], z(32),
        w[:, 6344:6600],
        w[:, 3584:3592], z(120),
        z(IN_COLS_PAD - 6784),
    ]
    return jnp.concatenate(parts, axis=1).astype(BF16)


def _col_spec(width, col, nl):
    blk = col // width
    assert blk * width == col
    return pl.BlockSpec((TIME_BLOCK, width), lambda b, l: (b * nl + l, blk))


def _row_spec(width):
    return pl.BlockSpec((1, width), lambda b, l: (0, 0))


def _full_spec(shape):
    return pl.BlockSpec(shape, lambda b, l: tuple(0 for _ in shape))


def _causal_conv4(buf_ref, x, w_ref, b_ref):
    tl = x.shape[0]
    buf_ref[pl.ds(SUBLANES, tl), :] = x
    y = b_ref[...] + w_ref[3:4, :] * x
    for j in range(3):
        y = y + w_ref[j:j + 1, :] * buf_ref[pl.ds(SUBLANES - 3 + j, tl), :]
    buf_ref[pl.ds(0, SUBLANES), :] = buf_ref[pl.ds(tl, SUBLANES), :]
    return y


def _rglru_body(gate_ref, x_ref, cw_ref, cb_ref, wa_ref, ba_ref, wx_ref, bx_ref, lam_ref,
                o_ref, xbuf, hcar):
    tl = TIME_BLOCK

    @pl.when(pl.program_id(1) == 0)
    def _():
        xbuf[pl.ds(0, SUBLANES), :] = jnp.zeros((SUBLANES, GW), F32)
        hcar[...] = jnp.zeros_like(hcar)

    xb = _causal_conv4(xbuf, x_ref[...], cw_ref, cb_ref)
    xb16 = xb.astype(BF16)
    r = _sigmoid(jnp.dot(xb16, wa_ref[...], preferred_element_type=F32) + ba_ref[...])
    i = _sigmoid(jnp.dot(xb16, wx_ref[...], preferred_element_type=F32) + bx_ref[...])
    log_a = -RG_C * r * _softplus(-lam_ref[...])
    a = jnp.exp(log_a)
    u = jnp.sqrt(-_expm1(2.0 * log_a)) * (i * xb)
    row = _iota2((tl, GW), 0)
    d = 1
    while d < tl:
        keep = row >= d
        a_s = jnp.where(keep, pltpu.roll(a, d, 0), 1.0)
        u_s = jnp.where(keep, pltpu.roll(u, d, 0), 0.0)
        u = a * u_s + u
        a = a * a_s
        d *= 2
    h = a * hcar[0:1, :] + u
    hcar[0:1, :] = h[tl - 1:tl, :]
    o_ref[...] = (h * jax.nn.gelu(gate_ref[...], approximate=True)).astype(BF16)


def _block_diag(w):
    nb, k, _ = w.shape
    eye = jnp.eye(nb, dtype=w.dtype)
    return (eye[:, None, :, None] * w[:, :, None, :]).reshape(nb * k, nb * k)


def _rglru(proj, bsz, seq, cw, cb, w_a, b_a, w_x, b_x, lam):
    nl = seq // TIME_BLOCK
    r = lambda v: v.reshape(1, GW)
    return pl.pallas_call(
        _rglru_body,
        grid=(bsz, nl),
        in_specs=[
            _col_spec(GW, C_RG_GATE, nl), _col_spec(GW, C_RG_X, nl),
            _full_spec((4, GW)), _row_spec(GW),
            _full_spec((GW, GW)), _row_spec(GW), _full_spec((GW, GW)), _row_spec(GW), _row_spec(GW),
        ],
        out_specs=pl.BlockSpec((TIME_BLOCK, GW), lambda b, l: (b * nl + l, 0)),
        out_shape=jax.ShapeDtypeStruct((bsz * seq, GW), BF16),
        scratch_shapes=[pltpu.VMEM((TIME_BLOCK + SUBLANES, GW), F32), pltpu.VMEM((SUBLANES, GW), F32)],
        compiler_params=_cparams(("arbitrary", "arbitrary")),
        name="rglru",
    )(proj, proj, cw, r(cb), _block_diag(w_a).astype(BF16), r(b_a), _block_diag(w_x).astype(BF16), r(b_x), r(lam))


SSD_CHUNK = 64


def _ssd_body(z_ref, x_ref, b_ref, c_ref, dt_ref, cwx_ref, cwb_ref, cwc_ref, cbx_ref, cbb_ref, cbc_ref,
              dtb_ref, alog_ref, dsk_ref, nw_ref, o_ref, xbuf, bbuf, cbuf, st_ref):
    tl, ck = TIME_BLOCK, SSD_CHUNK
    npair = SSM_HEADS // 2

    @pl.when(pl.program_id(1) == 0)
    def _():
        xbuf[pl.ds(0, SUBLANES), :] = jnp.zeros((SUBLANES, GW), F32)
        bbuf[pl.ds(0, SUBLANES), :] = jnp.zeros((SUBLANES, 2 * SSM_STATE), F32)
        cbuf[pl.ds(0, SUBLANES), :] = jnp.zeros((SUBLANES, 2 * SSM_STATE), F32)
        st_ref[...] = jnp.zeros_like(st_ref)

    xs_all = _silu(_causal_conv4(xbuf, x_ref[...], cwx_ref, cbx_ref))
    bm_all = _silu(_causal_conv4(bbuf, b_ref[...], cwb_ref, cbb_ref))
    cm_all = _silu(_causal_conv4(cbuf, c_ref[...], cwc_ref, cbc_ref))
    dt_all = _softplus(dt_ref[...] + dtb_ref[...])
    a_all = dt_all * (-jnp.exp(alog_ref[...]))

    expand = (_iota2((LANES, GW), 1) // SSM_HEAD_DIM == _iota2((LANES, GW), 0)).astype(BF16)
    tri = (_iota2((ck, ck), 0) >= _iota2((ck, ck), 1))
    tri16 = tri.astype(BF16)
    lane = _iota2((ck, LANES), 1)
    outs = []
    for c in range(tl // ck):
        sl = slice(c * ck, (c + 1) * ck)
        xs, bm, cm, dt, a = xs_all[sl], bm_all[sl], cm_all[sl], dt_all[sl], a_all[sl]
        acum = _dot_exact_lhs(tri16, a)
        acum_t = jnp.transpose(acum)
        acum_x = _dot_exact_rhs(acum, expand)
        dt_x = _dot_exact_rhs(dt, expand)
        aend_x = acum_x[ck - 1:ck, :]
        xdt = xs * dt_x
        xdec = xdt * jnp.exp(aend_x - acum_x)
        eacum = jnp.exp(acum_x)
        eend = jnp.exp(aend_x)
        ys = []
        for p in range(npair):
            g = (2 * p) // (SSM_HEADS // SSM_GROUPS)
            gs = slice(g * SSM_STATE, (g + 1) * SSM_STATE)
            ps = slice(p * LANES, (p + 1) * LANES)
            gmat = _dotb(cm[:, gs], bm[:, gs], NT)
            yd = []
            for hh in range(2):
                h = 2 * p + hh
                rel = acum[:, h:h + 1] - acum_t[h:h + 1, :]
                dec = jnp.exp(jnp.where(tri, rel, -jnp.inf))
                yd.append(_dotb(gmat * dec, xdt[:, ps]))
            y_diag = jnp.where(lane < SSM_HEAD_DIM, yd[0], yd[1])
            st = st_ref[p]
            y_off = _dotb(cm[:, gs], st) * eacum[:, ps]
            st_ref[p] = st * eend[:, ps] + _dotb(bm[:, gs], xdec[:, ps], TN)
            ys.append(y_diag + y_off)
        y = jnp.concatenate(ys, axis=1) + xs * dsk_ref[...]
        outs.append(y)
    y = jnp.concatenate(outs, axis=0) * _silu(z_ref[...])
    half = GW // SSM_GROUPS
    parts = []
    for g in range(SSM_GROUPS):
        yg = y[:, g * half:(g + 1) * half]
        parts.append(yg * lax.rsqrt(jnp.mean(yg * yg, axis=-1, keepdims=True) + NORM_EPS))
    o_ref[...] = (jnp.concatenate(parts, axis=1) * nw_ref[...]).astype(BF16)


def _ssd(proj, bsz, seq, conv_w, conv_b, dt_bias, a_log, d_skip, norm_w):
    nl = seq // TIME_BLOCK
    ns2 = 2 * SSM_STATE
    pad_row = lambda v: jnp.pad(v, (0, LANES - v.shape[0])).reshape(1, LANES)
    return pl.pallas_call(
        _ssd_body,
        grid=(bsz, nl),
        in_specs=[
            _col_spec(GW, C_SSM_Z, nl), _col_spec(GW, C_SSM_X, nl), _col_spec(ns2, C_SSM_B, nl),
            _col_spec(ns2, C_SSM_C, nl), _col_spec(LANES, C_SSM_DT, nl),
            _full_spec((4, GW)), _full_spec((4, ns2)), _full_spec((4, ns2)),
            _row_spec(GW), _row_spec(ns2), _row_spec(ns2),
            _row_spec(LANES), _row_spec(LANES), _row_spec(GW), _row_spec(GW),
        ],
        out_specs=pl.BlockSpec((TIME_BLOCK, GW), lambda b, l: (b * nl + l, 0)),
        out_shape=jax.ShapeDtypeStruct((bsz * seq, GW), BF16),
        scratch_shapes=[
            pltpu.VMEM((TIME_BLOCK + SUBLANES, GW), F32),
            pltpu.VMEM((TIME_BLOCK + SUBLANES, ns2), F32),
            pltpu.VMEM((TIME_BLOCK + SUBLANES, ns2), F32),
            pltpu.VMEM((SSM_HEADS // 2, SSM_STATE, LANES), F32),
        ],
        compiler_params=_cparams(("arbitrary", "arbitrary")),
        name="ssd",
    )(proj, proj, proj, proj, proj,
      conv_w[:, :GW], conv_w[:, GW:GW + ns2], conv_w[:, GW + ns2:],
      conv_b[:GW].reshape(1, GW), conv_b[GW:GW + ns2].reshape(1, ns2), conv_b[GW + ns2:].reshape(1, ns2),
      pad_row(dt_bias), pad_row(a_log), jnp.repeat(d_skip, SSM_HEAD_DIM).reshape(1, GW), norm_w.reshape(1, GW))


HG_CHUNK = 16


def _hgrn2_body(q_ref, f_ref, i_ref, g_ref, lb_ref, nw_ref, o_ref, st_ref):
    tl, ck = TIME_BLOCK, HG_CHUNK
    nh = GW // HG_HEAD

    @pl.when(pl.program_id(1) == 0)
    def _():
        st_ref[...] = jnp.zeros_like(st_ref)

    lb = lb_ref[...]
    row = _iota2((ck, GW), 0)
    tri16 = (_iota2((ck, ck), 0) >= _iota2((ck, ck), 1)).astype(BF16)

    def chunk(c, carry):
        r0 = pl.multiple_of(c * ck, ck)
        q = _silu(q_ref[pl.ds(r0, ck), :])
        f = lb + (1.0 - lb) * _sigmoid(f_ref[pl.ds(r0, ck), :])
        k = 1.0 - f
        v = i_ref[pl.ds(r0, ck), :]
        b = _dot_exact_lhs(tri16, jnp.log(f))
        bend = b[ck - 1:ck, :]
        qe = q * jnp.exp(b)
        kd = k * jnp.exp(bend - b)
        eend = jnp.exp(bend)
        acc = [jnp.zeros((ck, HG_HEAD), F32) for _ in range(nh)]
        for s in range(ck):
            w = q * k[s:s + 1, :] * jnp.exp(jnp.where(row >= s, b - b[s:s + 1, :], -jnp.inf))
            for h in range(nh):
                hs = slice(h * HG_HEAD, (h + 1) * HG_HEAD)
                acc[h] = acc[h] + jnp.sum(w[:, hs], axis=-1, keepdims=True) * v[s:s + 1, hs]
        outs = []
        for h in range(nh):
            hs = slice(h * HG_HEAD, (h + 1) * HG_HEAD)
            st = st_ref[h]
            o = acc[h] + _dotb(qe[:, hs], st, NT)
            st_ref[h] = st * eend[:, hs] + _dotb(v[:, hs], kd[:, hs], TN)
            outs.append(o * lax.rsqrt(jnp.mean(o * o, axis=-1, keepdims=True) + NORM_EPS))
        o = jnp.concatenate(outs, axis=1) * nw_ref[...] * _silu(g_ref[pl.ds(r0, ck), :])
        o_ref[pl.ds(r0, ck), :] = o.astype(BF16)
        return carry

    lax.fori_loop(0, tl // ck, chunk, 0)


def _hgrn2(proj, bsz, seq, lower_bound, norm_w):
    nl = seq // TIME_BLOCK
    return pl.pallas_call(
        _hgrn2_body,
        grid=(bsz, nl),
        in_specs=[
            _col_spec(GW, C_HG_Q, nl), _col_spec(GW, C_HG_F, nl), _col_spec(GW, C_HG_I, nl), _col_spec(GW, C_HG_G, nl),
            _row_spec(GW), _row_spec(GW),
        ],
        out_specs=pl.BlockSpec((TIME_BLOCK, GW), lambda b, l: (b * nl + l, 0)),
        out_shape=jax.ShapeDtypeStruct((bsz * seq, GW), BF16),
        scratch_shapes=[pltpu.VMEM((GW // HG_HEAD, HG_HEAD, HG_HEAD), F32)],
        compiler_params=_cparams(("arbitrary", "arbitrary")),
        name="hgrn2",
    )(proj, proj, proj, proj, lower_bound.reshape(1, GW), norm_w.reshape(1, GW))


RW_T = 32
RW_HP = 4
RW_GL = RW_HP * RW_HEAD_DIM
RW_NG = RW_HEADS // RW_HP


def _rwkv7_body(r_ref, k_ref, v_ref, wd_ref, ad_ref, gd_ref, mur_ref, muk_ref, muv_ref, muwd_ref, muad_ref, mugd_ref,
                w0_ref, w2_ref, a0_ref, a2_ref, g2_ref, kkw_ref, kaw_ref, rkw_ref, lnw_ref, lnb_ref,
                o_ref, car_r, car_k, car_v, car_wd, car_ad, car_gd, st_ref,
                kap_s, bet_s, kh_s, rho_s, v_s, c_s, y_s):
    tl, ct = TIME_BLOCK, RW_T
    srows = RW_HP * ct

    @pl.when(pl.program_id(1) == 0)
    def _():
        for car in (car_r, car_k, car_v, car_wd, car_ad, car_gd):
            car[...] = jnp.zeros_like(car)
        st_ref[...] = jnp.zeros_like(st_ref)

    def tshift(x_ref, car, mu_ref):
        x = x_ref[...]
        row = _iota2(x.shape, 0)
        prev = jnp.where(row == 0, car[0:1, :], pltpu.roll(x, 1, 0))
        car[0:1, :] = x[tl - 1:tl, :]
        return x + (prev - x) * mu_ref[...]

    r = tshift(r_ref, car_r, mur_ref)
    k = tshift(k_ref, car_k, muk_ref)
    v = tshift(v_ref, car_v, muv_ref)
    wd = tshift(wd_ref, car_wd, muwd_ref)
    ad = tshift(ad_ref, car_ad, muad_ref)
    gd = tshift(gd_ref, car_gd, mugd_ref)

    lw = -jnp.exp(-_softplus(-(w0_ref[...] + _dotb(jnp.tanh(wd), w2_ref[...]))) - 0.5)
    a = _sigmoid(a0_ref[...] + _dotb(ad, a2_ref[...]))
    g = _dotb(_sigmoid(gd), g2_ref[...])
    seg = (_iota2((GW, GW), 0) // RW_HEAD_DIM == _iota2((GW, GW), 1) // RW_HEAD_DIM).astype(BF16)
    kkr = k * kkw_ref[...]
    kk = kkr / jnp.maximum(jnp.sqrt(_dot_exact_rhs(kkr * kkr, seg)), 1e-12)
    k2 = k * (1.0 + (a - 1.0) * kaw_ref[...])
    blocktri = ((_iota2((tl, tl), 0) >= _iota2((tl, tl), 1))
                & (_iota2((tl, tl), 0) // ct == _iota2((tl, tl), 1) // ct)).astype(BF16)
    c = _dot_exact_lhs(blocktri, lw)
    enc = jnp.exp(-c)
    kap_s[...] = kk * jnp.exp(c - lw)
    bet_s[...] = kk * a * enc
    kh_s[...] = k2 * enc
    rho_s[...] = r * jnp.exp(c)
    v_s[...] = v
    c_s[...] = c

    rs = _iota2((srows, RW_GL), 0)
    hmask = (rs // ct) == (_iota2((srows, RW_GL), 1) // RW_HEAD_DIM)
    ri, cj = _iota2((srows, srows), 0), _iota2((srows, srows), 1)
    same = (ri // ct) == (cj // ct)
    strict = same & (ri > cj)
    incl = same & (ri >= cj)
    eye = (ri == cj).astype(F32)

    def stack(x):
        return jnp.where(hmask, jnp.concatenate([x] * RW_HP, axis=0), 0.0)

    def chunk(ci, carry):
        r0 = pl.multiple_of(ci * ct, ct)
        for gi in range(RW_NG):
            gl = slice(gi * RW_GL, (gi + 1) * RW_GL)
            cc = c_s[pl.ds(r0, ct), gl]
            cend = cc[ct - 1:ct, :]
            eup = jnp.exp(cend)
            kaps = stack(kap_s[pl.ds(r0, ct), gl])
            bet = bet_s[pl.ds(r0, ct), gl]
            kh = kh_s[pl.ds(r0, ct), gl]
            bets, khs = stack(bet), stack(kh)
            rhos = stack(rho_s[pl.ds(r0, ct), gl])
            vs = stack(v_s[pl.ds(r0, ct), gl])
            a_ab = jnp.where(strict, _dot3(kaps, bets, NT), 0.0)
            a_ak = jnp.where(strict, _dot3(kaps, khs, NT), 0.0)
            a_rb = jnp.where(incl, _dot3(rhos, bets, NT), 0.0)
            a_rk = jnp.where(incl, _dot3(rhos, khs, NT), 0.0)
            pw = -a_ab
            inv = eye + pw
            for _ in range(int(math.log2(ct)) - 1):
                pw = _dot3(pw, pw)
                inv = inv + _dot3(inv, pw)
            s0 = st_ref[gi]
            u = -_dot3(inv, _dot3(kaps, s0, NT) + _dot3(a_ak, vs))
            ys = _dot3(rhos, s0, NT) + _dot3(a_rb, u) + _dot3(a_rk, vs)
            y = ys[0:ct]
            for hh in range(1, RW_HP):
                y = y + ys[hh * ct:(hh + 1) * ct]
            y_s[pl.ds(r0, ct), gl] = y
            st_ref[gi] = s0 * eup + _dot3(u, bets * eup, TN) + _dot3(vs, khs * eup, TN)
        return carry

    lax.fori_loop(0, tl // ct, chunk, 0)

    y = y_s[...]
    inv_n = 1.0 / RW_HEAD_DIM
    mean = _dot_exact_rhs(y, seg) * inv_n
    yc = y - mean
    var = _dot_exact_rhs(yc * yc, seg) * inv_n
    yn = yc * lax.rsqrt(var + RW_GN_EPS) * lnw_ref[...] + lnb_ref[...]
    bonus = _dot_exact_rhs(r * k2 * rkw_ref[...], seg) * v
    o_ref[...] = ((yn + bonus) * g).astype(BF16)


def _rwkv7(proj, bsz, seq, mu, w0, w2, a0, a2, g2, k_k, k_a, r_k, ln_w, ln_b):
    nl = seq // TIME_BLOCK
    row = lambda v_: v_.reshape(1, -1)
    padl = lambda v_: jnp.pad(v_, (0, LANES - v_.shape[0])).reshape(1, LANES)
    padr = lambda m: jnp.pad(m, ((0, LANES - m.shape[0]), (0, 0))).astype(BF16)
    o = 3 * GW
    mus = [row(mu[:GW]), row(mu[GW:2 * GW]), row(mu[2 * GW:o]), padl(mu[o:o + RW_LORA]),
           padl(mu[o + RW_LORA:o + 2 * RW_LORA]), row(mu[o + 2 * RW_LORA:])]
    f32buf = lambda w: pltpu.VMEM((TIME_BLOCK, w), F32)
    car = lambda w: pltpu.VMEM((SUBLANES, w), F32)
    return pl.pallas_call(
        _rwkv7_body,
        grid=(bsz, nl),
        in_specs=[
            _col_spec(GW, C_RW_R, nl), _col_spec(GW, C_RW_K, nl), _col_spec(GW, C_RW_V, nl),
            _col_spec(LANES, C_RW_WD, nl), _col_spec(LANES, C_RW_AD, nl), _col_spec(RW_GATE, C_RW_GD, nl),
            _row_spec(GW), _row_spec(GW), _row_spec(GW), _row_spec(LANES), _row_spec(LANES), _row_spec(RW_GATE),
            _row_spec(GW), _full_spec((LANES, GW)), _row_spec(GW), _full_spec((LANES, GW)), _full_spec((RW_GATE, GW)),
            _row_spec(GW), _row_spec(GW), _row_spec(GW), _row_spec(GW), _row_spec(GW),
        ],
        out_specs=pl.BlockSpec((TIME_BLOCK, GW), lambda b, l: (b * nl + l, 0)),
        out_shape=jax.ShapeDtypeStruct((bsz * seq, GW), BF16),
        scratch_shapes=[car(GW), car(GW), car(GW), car(LANES), car(LANES), car(RW_GATE),
                        pltpu.VMEM((RW_NG, RW_GL, RW_GL), F32)] + [f32buf(GW)] * 7,
        compiler_params=_cparams(("arbitrary", "arbitrary")),
        name="rwkv7",
    )(proj, proj, proj, proj, proj, proj, *mus,
      row(w0), padr(w2), row(a0), padr(a2), g2.astype(BF16), row(k_k), row(k_a), row(r_k), row(ln_w), row(ln_b))


ROUTE_TM = 256
R_E1, R_E2, R_G1, R_G2, R_K1, R_K2 = 0, 1, 2, 3, 4, 5


def _outproj_router_body(h_ref, ya_ref, yb_ref, yc_ref, yd_ref, wo_ref, nw_ref, wr_ref, br_ref,
                         ho_ref, xn_ref, route_ref, cnt_ref, cnt_s):
    tm = ROUTE_TM

    @pl.when(pl.program_id(0) == 0)
    def _():
        cnt_s[...] = jnp.zeros_like(cnt_s)

    acc = h_ref[...]
    for gi, y_ref in enumerate((ya_ref, yb_ref, yc_ref, yd_ref)):
        acc = acc + jnp.dot(y_ref[...], wo_ref[gi * GW:(gi + 1) * GW, :], preferred_element_type=F32)
    ho_ref[...] = acc
    xn = acc * lax.rsqrt(jnp.mean(acc * acc, axis=-1, keepdims=True) + NORM_EPS) * nw_ref[...]
    xn_ref[...] = xn
    logits = _dot3(xn, wr_ref[...]) + br_ref[...]

    lane = _iota2((tm, LANES), 1)
    neg = -jnp.inf
    gl = jnp.where(lane < N_GROUPS, logits, neg)
    mg = jnp.max(gl, axis=-1, keepdims=True)
    g_top = 1.0 / jnp.sum(jnp.exp(gl - mg), axis=-1, keepdims=True)
    g_idx = jnp.min(jnp.where(gl == mg, lane, LANES), axis=-1, keepdims=True)
    lo = N_GROUPS + E_PER_GROUP * g_idx
    el = jnp.where(lane >= lo, jnp.where(lane < lo + E_PER_GROUP, logits, neg), neg)
    me = jnp.max(el, axis=-1, keepdims=True)
    se = jnp.sum(jnp.exp(el - me), axis=-1, keepdims=True)
    i1 = jnp.min(jnp.where(el == me, lane, LANES), axis=-1, keepdims=True)
    el2 = jnp.where(lane == i1, neg, el)
    m2 = jnp.max(el2, axis=-1, keepdims=True)
    i2 = jnp.min(jnp.where(el2 == m2, lane, LANES), axis=-1, keepdims=True)
    p1 = 1.0 / se
    p2 = jnp.exp(m2 - me) / se
    gate1 = g_top * p1 / (p1 + p2)
    gate2 = g_top * p2 / (p1 + p2)
    e1 = i1 - N_GROUPS
    e2 = i2 - N_GROUPS

    oh1 = lane == e1
    oh2 = lane == e2
    oh = jnp.where(oh1, 1.0, jnp.where(oh2, 1.0, 0.0))
    stri = (_iota2((tm, tm), 0) > _iota2((tm, tm), 1)).astype(BF16)
    before = jnp.dot(stri, oh.astype(BF16), preferred_element_type=F32) + cnt_s[0:1, :]
    k1 = jnp.sum(jnp.where(oh1, before, 0.0), axis=-1, keepdims=True)
    k2 = jnp.sum(jnp.where(oh2, before, 0.0), axis=-1, keepdims=True)
    cnt = cnt_s[0:1, :] + jnp.sum(oh, axis=0, keepdims=True)
    cnt_s[0:1, :] = cnt
    cnt_ref[...] = jnp.broadcast_to(cnt, cnt_ref.shape)

    rec = jnp.where(lane == R_E1, e1.astype(F32), 0.0)
    rec = jnp.where(lane == R_E2, e2.astype(F32), rec)
    rec = jnp.where(lane == R_G1, gate1, rec)
    rec = jnp.where(lane == R_G2, gate2, rec)
    rec = jnp.where(lane == R_K1, k1, rec)
    rec = jnp.where(lane == R_K2, k2, rec)
    route_ref[...] = rec


def _outproj_router(h, ys, w_out16, norm_w, wr, br):
    n = h.shape[0]
    tm = ROUTE_TM
    rows = lambda w: pl.BlockSpec((tm, w), lambda i: (i, 0))
    whole = lambda shape: pl.BlockSpec(shape, lambda i: (0, 0))
    return pl.pallas_call(
        _outproj_router_body,
        grid=(n // tm,),
        in_specs=[rows(D_MODEL), rows(GW), rows(GW), rows(GW), rows(GW), whole((D_MODEL, D_MODEL)),
                  whole((1, D_MODEL)), whole((D_MODEL, LANES)), whole((1, LANES))],
        out_specs=[rows(D_MODEL), rows(D_MODEL), rows(LANES), whole((SUBLANES, LANES))],
        out_shape=[jax.ShapeDtypeStruct((n, D_MODEL), F32), jax.ShapeDtypeStruct((n, D_MODEL), F32),
                   jax.ShapeDtypeStruct((n, LANES), F32), jax.ShapeDtypeStruct((SUBLANES, LANES), F32)],
        scratch_shapes=[pltpu.VMEM((SUBLANES, LANES), F32)],
        compiler_params=_cparams(("arbitrary",)),
        name="outproj_router",
    )(h, *ys, w_out16, norm_w.reshape(1, D_MODEL), wr, br)


MOE_BLK = 256


def _moe_n_blocks(n_assign):
    return (n_assign + N_EXPERTS * (MOE_BLK - 1) + MOE_BLK - 1) // MOE_BLK


def _ffn_body(bexp_ref, stok_ref, sdst_ref, nused_ref, xn_hbm, wg_ref, wu_ref, wd_ref, y_hbm,
              xbuf, obuf, gsem, ssem):
    b = pl.program_id(0)
    nb = pl.num_programs(0)
    nused = nused_ref[0]
    blk = MOE_BLK

    def gather_start(j, slot):
        def row(r, c):
            t = stok_ref[j * blk + r]
            pltpu.make_async_copy(xn_hbm.at[pl.ds(t, 1)], xbuf.at[slot, pl.ds(r, 1)], gsem.at[slot]).start()
            return c
        lax.fori_loop(0, blk, row, 0, unroll=8)

    def gather_wait(slot):
        pltpu.make_async_copy(xn_hbm.at[pl.ds(0, blk)], xbuf.at[slot], gsem.at[slot]).wait()

    def scatter_rows(j, slot, wait):
        def row(r, c):
            a = sdst_ref[j * blk + r]

            @pl.when(a >= 0)
            def _():
                cp = pltpu.make_async_copy(obuf.at[slot, pl.ds(r, 1)], y_hbm.at[pl.ds(jnp.maximum(a, 0), 1)], ssem.at[slot])
                if wait:
                    cp.wait()
                else:
                    cp.start()
            return c
        lax.fori_loop(0, blk, row, 0, unroll=8)

    @pl.when(b == 0)
    def _():
        gather_start(0, 0)

    @pl.when(jnp.logical_and(b >= 2, b - 2 < nused))
    def _():
        scatter_rows(b - 2, b % 2, True)

    @pl.when(b < nused)
    def _():
        slot = b % 2
        gather_wait(slot)

        @pl.when(b + 1 < nused)
        def _():
            gather_start(b + 1, 1 - slot)

        x = xbuf[slot].astype(BF16)
        hid = _silu(jnp.dot(x, wg_ref[0], preferred_element_type=F32)) * jnp.dot(x, wu_ref[0], preferred_element_type=F32)
        obuf[slot] = jnp.dot(hid.astype(BF16), wd_ref[0], preferred_element_type=F32)
        scatter_rows(b, slot, False)

    @pl.when(b == nb - 1)
    def _():
        @pl.when(jnp.logical_and(b >= 1, b - 1 < nused))
        def _():
            scatter_rows(b - 1, (b - 1) % 2, True)

        @pl.when(b < nused)
        def _():
            scatter_rows(b, b % 2, True)


def _moe_ffn(xn, wg16, wu16, wd16, bexp, stok, sdst, nused):
    n = xn.shape[0]
    n_blocks = bexp.shape[0]
    wspec = lambda shape: pl.BlockSpec((1,) + shape, lambda b, be, st, sd, nu: (be[b], 0, 0))
    return pl.pallas_call(
        _ffn_body,
        grid_spec=pltpu.PrefetchScalarGridSpec(
            num_scalar_prefetch=4,
            grid=(n_blocks,),
            in_specs=[pl.BlockSpec(memory_space=pl.ANY), wspec((D_MODEL, D_EXPERT)), wspec((D_MODEL, D_EXPERT)),
                      wspec((D_EXPERT, D_MODEL))],
            out_specs=pl.BlockSpec(memory_space=pl.ANY),
            scratch_shapes=[pltpu.VMEM((2, MOE_BLK, D_MODEL), F32), pltpu.VMEM((2, MOE_BLK, D_MODEL), F32),
                            pltpu.SemaphoreType.DMA((2,)), pltpu.SemaphoreType.DMA((2,))],
        ),
        out_shape=jax.ShapeDtypeStruct((2 * n, D_MODEL), F32),
        compiler_params=_cparams(("arbitrary",)),
        name="moe_ffn",
    )(bexp, stok, sdst, nused, xn, wg16, wu16, wd16)


def _dispatch_plan(route, counts, n_blocks):
    n = route.shape[0]
    eid = route[:, R_E1:R_E2 + 1].astype(jnp.int32)
    rank = route[:, R_K1:R_K2 + 1].astype(jnp.int32)
    cnt = counts[0, :N_EXPERTS].astype(jnp.int32)
    padded = (cnt + MOE_BLK - 1) // MOE_BLK * MOE_BLK
    pad_end = jnp.cumsum(padded)
    pad_start = pad_end - padded
    slot = (pad_start[eid] + rank).reshape(-1)
    n_slots = n_blocks * MOE_BLK
    asg = jnp.arange(2 * n, dtype=jnp.int32)
    sdst = jnp.full((n_slots,), -1, jnp.int32).at[slot].set(asg)
    stok = jnp.maximum(sdst, 0) // 2
    bexp = jnp.minimum(jnp.searchsorted(pad_end, jnp.arange(n_blocks, dtype=jnp.int32) * MOE_BLK, side='right'), N_EXPERTS - 1)
    nused = (pad_end[-1] // MOE_BLK).reshape(1).astype(jnp.int32)
    return bexp.astype(jnp.int32), stok, sdst, nused


def _combine_body(h_ref, y_ref, route_ref, nw_ref, o_ref, *, final):
    rt = route_ref[...]
    out = h_ref[...] + rt[:, R_G1:R_G1 + 1] * y_ref[:, :D_MODEL] + rt[:, R_G2:R_G2 + 1] * y_ref[:, D_MODEL:]
    if final:
        out = out * lax.rsqrt(jnp.mean(out * out, axis=-1, keepdims=True) + NORM_EPS) * nw_ref[...]
    o_ref[...] = out


def _combine(h, y2, route, norm_w, final, tm=256):
    n = h.shape[0]
    return pl.pallas_call(
        functools.partial(_combine_body, final=final),
        grid=(n // tm,),
        in_specs=[pl.BlockSpec((tm, D_MODEL), lambda i: (i, 0)), pl.BlockSpec((tm, 2 * D_MODEL), lambda i: (i, 0)),
                  pl.BlockSpec((tm, LANES), lambda i: (i, 0)), pl.BlockSpec((1, D_MODEL), lambda i: (0, 0))],
        out_specs=pl.BlockSpec((tm, D_MODEL), lambda i: (i, 0)),
        out_shape=jax.ShapeDtypeStruct((n, D_MODEL), F32),
        compiler_params=_cparams(("arbitrary",)),
        name="combine",
    )(h, y2.reshape(n, 2 * D_MODEL), route, norm_w.reshape(1, D_MODEL))


def kernel(x, w_in, w_out, norm_mix_w, norm_ffn_w, final_norm_w, hg_lb_param, hg_norm_w, ssm_conv_w, ssm_conv_b, ssm_dt_bias, ssm_a_log, ssm_d, ssm_norm_w, rg_conv_w, rg_conv_b, rg_w_a, rg_b_a, rg_w_x, rg_b_x, rg_lambda, rw_mu, rw_w0, rw_w2, rw_a0, rw_a2, rw_g2, rw_k_k, rw_k_a, rw_r_k, rw_ln_w, rw_ln_b, router_group_w, router_group_b, router_expert_w, router_expert_b, moe_w_gate, moe_w_up, moe_w_down):
    bsz, seq, d = x.shape
    n = bsz * seq
    n_blocks = _moe_n_blocks(2 * n)
    sm = jax.nn.softmax(hg_lb_param.astype(F32), axis=0)
    lower_bounds = jnp.cumsum(sm, axis=0) - sm[0]
    h = x.reshape(n, d)
    for l in range(w_in.shape[0]):
        proj = _inproj(h, norm_mix_w[l], _relayout_w_in(w_in[l]))
        ya = _hgrn2(proj, bsz, seq, lower_bounds[l], hg_norm_w[l])
        yb = _ssd(proj, bsz, seq, ssm_conv_w[l], ssm_conv_b[l], ssm_dt_bias[l], ssm_a_log[l], ssm_d[l], ssm_norm_w[l])
        yc = _rglru(proj, bsz, seq, rg_conv_w[l], rg_conv_b[l], rg_w_a[l], rg_b_a[l], rg_w_x[l], rg_b_x[l], rg_lambda[l])
        yd = _rwkv7(proj, bsz, seq, rw_mu[l], rw_w0[l], rw_w2[l], rw_a0[l], rw_a2[l], rw_g2[l], rw_k_k[l], rw_k_a[l],
                    rw_r_k[l], rw_ln_w[l], rw_ln_b[l])
        wr = jnp.pad(jnp.concatenate([router_group_w[l], router_expert_w[l]], axis=1),
                     ((0, 0), (0, LANES - N_GROUPS - N_EXPERTS)))
        br = jnp.pad(jnp.concatenate([router_group_b[l], router_expert_b[l]]), (0, LANES - N_GROUPS - N_EXPERTS)).reshape(1, LANES)
        h, xn, route, counts = _outproj_router(h, (ya, yb, yc, yd), w_out[l].astype(BF16), norm_ffn_w[l], wr, br)
        bexp, stok, sdst, nused = _dispatch_plan(route, counts, n_blocks)
        y2 = _moe_ffn(xn, moe_w_gate[l].astype(BF16), moe_w_up[l].astype(BF16), moe_w_down[l].astype(BF16),
                      bexp, stok, sdst, nused)
        h = _combine(h, y2, route, final_norm_w, final=(l == w_in.shape[0] - 1))
    return h.reshape(bsz, seq, d)
```

```python
import functools
import math

import jax
import jax.numpy as jnp
from jax import lax
from jax.experimental import pallas as pl
from jax.experimental.pallas import tpu as pltpu

F32 = jnp.float32
BF16 = jnp.bfloat16

D_MODEL = 2048
GW = 512
NORM_EPS = 1e-6
HG_HEAD = 128
SSM_HEADS = 8
SSM_HEAD_DIM = 64
SSM_STATE = 128
SSM_GROUPS = 2
RG_C = 8.0
RW_HEADS = 8
RW_HEAD_DIM = 64
RW_LORA = 96
RW_GATE = 256
RW_GN_EPS = 64e-5
N_GROUPS = 4
E_PER_GROUP = 8
N_EXPERTS = 32
D_EXPERT = 1024

LANES = 128
SUBLANES = 8
VMEM_LIMIT = 56 * 1024 * 1024

C_HG_Q, C_HG_F, C_HG_I, C_HG_G = 0, 512, 1024, 1536
C_SSM_Z, C_SSM_X, C_SSM_B, C_SSM_C = 2048, 2560, 3072, 3328
C_RG_GATE, C_RG_X = 3584, 4096
C_RW_R, C_RW_K, C_RW_V = 4608, 5120, 5632
C_RW_WD, C_RW_AD, C_RW_GD = 6144, 6272, 6400
C_SSM_DT = 6656
IN_COLS_PAD = 6912

TIME_BLOCK = 256


def _cparams(sem):
    return pltpu.CompilerParams(dimension_semantics=sem, vmem_limit_bytes=VMEM_LIMIT)


def _split3(x):
    hi = x.astype(BF16)
    r1 = x - hi.astype(F32)
    mid = r1.astype(BF16)
    lo = (r1 - mid.astype(F32)).astype(BF16)
    return hi, mid, lo


def _dot_exact_rhs(x, m_bf16, dims=(((1,), (0,)), ((), ()))):
    hi, mid, lo = _split3(x)
    f = lambda a: lax.dot_general(a, m_bf16, dims, preferred_element_type=F32)
    return f(hi) + f(mid) + f(lo)


def _dot_exact_lhs(m_bf16, x, dims=(((1,), (0,)), ((), ()))):
    hi, mid, lo = _split3(x)
    f = lambda a: lax.dot_general(m_bf16, a, dims, preferred_element_type=F32)
    return f(hi) + f(mid) + f(lo)


def _dot3(a, b, dims=(((1,), (0,)), ((), ()))):
    ah = a.astype(BF16)
    al = (a - ah.astype(F32)).astype(BF16)
    bh = b.astype(BF16)
    bl = (b - bh.astype(F32)).astype(BF16)
    f = lambda p, q: lax.dot_general(p, q, dims, preferred_element_type=F32)
    return f(ah, bh) + f(ah, bl) + f(al, bh)


def _dotb(a, b, dims=(((1,), (0,)), ((), ()))):
    return lax.dot_general(a.astype(BF16), b.astype(BF16), dims, preferred_element_type=F32)


NT = (((1,), (1,)), ((), ()))
TN = (((0,), (0,)), ((), ()))


def _sigmoid(x):
    return 1.0 / (1.0 + jnp.exp(-x))


def _silu(x):
    return x * _sigmoid(x)


def _softplus(x):
    return jnp.maximum(x, 0.0) + jnp.log1p(jnp.exp(-jnp.abs(x)))


def _expm1(x):
    u = jnp.exp(x)
    return jnp.where(u == 1.0, x, jnp.where(u == 0.0, -1.0, (u - 1.0) * x / jnp.log(u)))


def _iota2(shape, axis):
    return lax.broadcasted_iota(jnp.int32, shape, axis)


def _inproj_body(x_ref, nw_ref, w_ref, o_ref, u_ref):
    @pl.when(pl.program_id(1) == 0)
    def _():
        x = x_ref[...]
        ms = jnp.mean(x * x, axis=-1, keepdims=True)
        u_ref[...] = (x * lax.rsqrt(ms + NORM_EPS) * nw_ref[...]).astype(BF16)

    o_ref[...] = jnp.dot(u_ref[...], w_ref[...], preferred_element_type=F32)


def _inproj(h, norm_w, w_pad, tm=1024, tn=768):
    n = h.shape[0]
    return pl.pallas_call(
        _inproj_body,
        grid=(n // tm, IN_COLS_PAD // tn),
        in_specs=[
            pl.BlockSpec((tm, D_MODEL), lambda i, j: (i, 0)),
            pl.BlockSpec((1, D_MODEL), lambda i, j: (0, 0)),
            pl.BlockSpec((D_MODEL, tn), lambda i, j: (0, j)),
        ],
        out_specs=pl.BlockSpec((tm, tn), lambda i, j: (i, j)),
        out_shape=jax.ShapeDtypeStruct((n, IN_COLS_PAD), F32),
        scratch_shapes=[pltpu.VMEM((tm, D_MODEL), BF16)],
        compiler_params=_cparams(("arbitrary", "arbitrary")),
        name="inproj",
    )(h, norm_w.reshape(1, D_MODEL), w_pad)


def _relayout_w_in(w):
    z = lambda c: jnp.zeros((w.shape[0], c), w.dtype)
    parts = [
        w[:, :3584],
        w[:, 3592:4616],
        w[:, 4616:6152],
        w[:, 6152:6248], z(32),
        w[:, 6248:6344], z(32),
        w[:, 6344:6600],
        w[:, 3584:3592], z(120),
        z(IN_COLS_PAD - 6784),
    ]
    return jnp.concatenate(parts, axis=1).astype(BF16)


def _col_spec(width, col, nl):
    blk = col // width
    assert blk * width == col
    return pl.BlockSpec((TIME_BLOCK, width), lambda b, l: (b * nl + l, blk))


def _row_spec(width):
    return pl.BlockSpec((1, width), lambda b, l: (0, 0))


def _full_spec(shape):
    return pl.BlockSpec(shape, lambda b, l: tuple(0 for _ in shape))


def _causal_conv4(buf_ref, x, w_ref, b_ref):
    tl = x.shape[0]
    buf_ref[pl.ds(SUBLANES, tl), :] = x
    y = b_ref[...] + w_ref[3:4, :] * x
    for j in range(3):
        y = y + w_ref[j:j + 1, :] * buf_ref[pl.ds(SUBLANES - 3 + j, tl), :]
    buf_ref[pl.ds(0, SUBLANES), :] = buf_ref[pl.ds(tl, SUBLANES), :]
    return y


def _rglru_body(gate_ref, x_ref, cw_ref, cb_ref, wa_ref, ba_ref, wx_ref, bx_ref, lam_ref,
                o_ref, xbuf, hcar):
    tl = TIME_BLOCK

    @pl.when(pl.program_id(1) == 0)
    def _():
        xbuf[pl.ds(0, SUBLANES), :] = jnp.zeros((SUBLANES, GW), F32)
        hcar[...] = jnp.zeros_like(hcar)

    xb = _causal_conv4(xbuf, x_ref[...], cw_ref, cb_ref)
    xb16 = xb.astype(BF16)
    r = _sigmoid(jnp.dot(xb16, wa_ref[...], preferred_element_type=F32) + ba_ref[...])
    i = _sigmoid(jnp.dot(xb16, wx_ref[...], preferred_element_type=F32) + bx_ref[...])
    log_a = -RG_C * r * _softplus(-lam_ref[...])
    a = jnp.exp(log_a)
    u = jnp.sqrt(-_expm1(2.0 * log_a)) * (i * xb)
    row = _iota2((tl, GW), 0)
    d = 1
    while d < tl:
        keep = row >= d
        a_s = jnp.where(keep, pltpu.roll(a, d, 0), 1.0)
        u_s = jnp.where(keep, pltpu.roll(u, d, 0), 0.0)
        u = a * u_s + u
        a = a * a_s
        d *= 2
    h = a * hcar[0:1, :] + u
    hcar[0:1, :] = h[tl - 1:tl, :]
    o_ref[...] = (h * jax.nn.gelu(gate_ref[...], approximate=True)).astype(BF16)


def _block_diag(w):
    nb, k, _ = w.shape
    eye = jnp.eye(nb, dtype=w.dtype)
    return (eye[:, None, :, None] * w[:, :, None, :]).reshape(nb * k, nb * k)


def _rglru(proj, bsz, seq, cw, cb, w_a, b_a, w_x, b_x, lam):
    nl = seq // TIME_BLOCK
    r = lambda v: v.reshape(1, GW)
    return pl.pallas_call(
        _rglru_body,
        grid=(bsz, nl),
        in_specs=[
            _col_spec(GW, C_RG_GATE, nl), _col_spec(GW, C_RG_X, nl),
            _full_spec((4, GW)), _row_spec(GW),
            _full_spec((GW, GW)), _row_spec(GW), _full_spec((GW, GW)), _row_spec(GW), _row_spec(GW),
        ],
        out_specs=pl.BlockSpec((TIME_BLOCK, GW), lambda b, l: (b * nl + l, 0)),
        out_shape=jax.ShapeDtypeStruct((bsz * seq, GW), BF16),
        scratch_shapes=[pltpu.VMEM((TIME_BLOCK + SUBLANES, GW), F32), pltpu.VMEM((SUBLANES, GW), F32)],
        compiler_params=_cparams(("arbitrary", "arbitrary")),
        name="rglru",
    )(proj, proj, cw, r(cb), _block_diag(w_a).astype(BF16), r(b_a), _block_diag(w_x).astype(BF16), r(b_x), r(lam))


SSD_CHUNK = 64


def _ssd_body(z_ref, x_ref, b_ref, c_ref, dt_ref, cwx_ref, cwb_ref, cwc_ref, cbx_ref, cbb_ref, cbc_ref,
              dtb_ref, alog_ref, dsk_ref, nw_ref, o_ref, xbuf, bbuf, cbuf, st_ref):
    tl, ck = TIME_BLOCK, SSD_CHUNK
    npair = SSM_HEADS // 2

    @pl.when(pl.program_id(1) == 0)
    def _():
        xbuf[pl.ds(0, SUBLANES), :] = jnp.zeros((SUBLANES, GW), F32)
        bbuf[pl.ds(0, SUBLANES), :] = jnp.zeros((SUBLANES, 2 * SSM_STATE), F32)
        cbuf[pl.ds(0, SUBLANES), :] = jnp.zeros((SUBLANES, 2 * SSM_STATE), F32)
        st_ref[...] = jnp.zeros_like(st_ref)

    xs_all = _silu(_causal_conv4(xbuf, x_ref[...], cwx_ref, cbx_ref))
    bm_all = _silu(_causal_conv4(bbuf, b_ref[...], cwb_ref, cbb_ref))
    cm_all = _silu(_causal_conv4(cbuf, c_ref[...], cwc_ref, cbc_ref))
    dt_all = _softplus(dt_ref[...] + dtb_ref[...])
    a_all = dt_all * (-jnp.exp(alog_ref[...]))

    expand = (_iota2((LANES, GW), 1) // SSM_HEAD_DIM == _iota2((LANES, GW), 0)).astype(BF16)
    tri = (_iota2((ck, ck), 0) >= _iota2((ck, ck), 1))
    tri16 = tri.astype(BF16)
    lane = _iota2((ck, LANES), 1)
    outs = []
    for c in range(tl // ck):
        sl = slice(c * ck, (c + 1) * ck)
        xs, bm, cm, dt, a = xs_all[sl], bm_all[sl], cm_all[sl], dt_all[sl], a_all[sl]
        acum = _dot_exact_lhs(tri16, a)
        acum_t = jnp.transpose(acum)
        acum_x = _dot_exact_rhs(acum, expand)
        dt_x = _dot_exact_rhs(dt, expand)
        aend_x = acum_x[ck - 1:ck, :]
        xdt = xs * dt_x
        xdec = xdt * jnp.exp(aend_x - acum_x)
        eacum = jnp.exp(acum_x)
        eend = jnp.exp(aend_x)
        ys = []
        for p in range(npair):
            g = (2 * p) // (SSM_HEADS // SSM_GROUPS)
            gs = slice(g * SSM_STATE, (g + 1) * SSM_STATE)
            ps = slice(p * LANES, (p + 1) * LANES)
            gmat = _dotb(cm[:, gs], bm[:, gs], NT)
            yd = []
            for hh in range(2):
                h = 2 * p + hh
                rel = acum[:, h:h + 1] - acum_t[h:h + 1, :]
                dec = jnp.exp(jnp.where(tri, rel, -jnp.inf))
                yd.append(_dotb(gmat * dec, xdt[:, ps]))
            y_diag = jnp.where(lane < SSM_HEAD_DIM, yd[0], yd[1])
            st = st_ref[p]
            y_off = _dotb(cm[:, gs], st) * eacum[:, ps]
            st_ref[p] = st * eend[:, ps] + _dotb(bm[:, gs], xdec[:, ps], TN)
            ys.append(y_diag + y_off)
        y = jnp.concatenate(ys, axis=1) + xs * dsk_ref[...]
        outs.append(y)
    y = jnp.concatenate(outs, axis=0) * _silu(z_ref[...])
    half = GW // SSM_GROUPS
    parts = []
    for g in range(SSM_GROUPS):
        yg = y[:, g * half:(g + 1) * half]
        parts.append(yg * lax.rsqrt(jnp.mean(yg * yg, axis=-1, keepdims=True) + NORM_EPS))
    o_ref[...] = (jnp.concatenate(parts, axis=1) * nw_ref[...]).astype(BF16)


def _ssd(proj, bsz, seq, conv_w, conv_b, dt_bias, a_log, d_skip, norm_w):
    nl = seq // TIME_BLOCK
    ns2 = 2 * SSM_STATE
    pad_row = lambda v: jnp.pad(v, (0, LANES - v.shape[0])).reshape(1, LANES)
    return pl.pallas_call(
        _ssd_body,
        grid=(bsz, nl),
        in_specs=[
            _col_spec(GW, C_SSM_Z, nl), _col_spec(GW, C_SSM_X, nl), _col_spec(ns2, C_SSM_B, nl),
            _col_spec(ns2, C_SSM_C, nl), _col_spec(LANES, C_SSM_DT, nl),
            _full_spec((4, GW)), _full_spec((4, ns2)), _full_spec((4, ns2)),
            _row_spec(GW), _row_spec(ns2), _row_spec(ns2),
            _row_spec(LANES), _row_spec(LANES), _row_spec(GW), _row_spec(GW),
        ],
        out_specs=pl.BlockSpec((TIME_BLOCK, GW), lambda b, l: (b * nl + l, 0)),
        out_shape=jax.ShapeDtypeStruct((bsz * seq, GW), BF16),
        scratch_shapes=[
            pltpu.VMEM((TIME_BLOCK + SUBLANES, GW), F32),
            pltpu.VMEM((TIME_BLOCK + SUBLANES, ns2), F32),
            pltpu.VMEM((TIME_BLOCK + SUBLANES, ns2), F32),
            pltpu.VMEM((SSM_HEADS // 2, SSM_STATE, LANES), F32),
        ],
        compiler_params=_cparams(("arbitrary", "arbitrary")),
        name="ssd",
    )(proj, proj, proj, proj, proj,
      conv_w[:, :GW], conv_w[:, GW:GW + ns2], conv_w[:, GW + ns2:],
      conv_b[:GW].reshape(1, GW), conv_b[GW:GW + ns2].reshape(1, ns2), conv_b[GW + ns2:].reshape(1, ns2),
      pad_row(dt_bias), pad_row(a_log), jnp.repeat(d_skip, SSM_HEAD_DIM).reshape(1, GW), norm_w.reshape(1, GW))


HG_CHUNK = 16


def _hgrn2_body(q_ref, f_ref, i_ref, g_ref, lb_ref, nw_ref, o_ref, st_ref):
    tl, ck = TIME_BLOCK, HG_CHUNK
    nh = GW // HG_HEAD

    @pl.when(pl.program_id(1) == 0)
    def _():
        st_ref[...] = jnp.zeros_like(st_ref)

    lb = lb_ref[...]
    row = _iota2((ck, GW), 0)
    tri16 = (_iota2((ck, ck), 0) >= _iota2((ck, ck), 1)).astype(BF16)

    def chunk(c, carry):
        r0 = pl.multiple_of(c * ck, ck)
        q = _silu(q_ref[pl.ds(r0, ck), :])
        f = lb + (1.0 - lb) * _sigmoid(f_ref[pl.ds(r0, ck), :])
        k = 1.0 - f
        v = i_ref[pl.ds(r0, ck), :]
        b = _dot_exact_lhs(tri16, jnp.log(f))
        bend = b[ck - 1:ck, :]
        qe = q * jnp.exp(b)
        kd = k * jnp.exp(bend - b)
        eend = jnp.exp(bend)
        acc = [jnp.zeros((ck, HG_HEAD), F32) for _ in range(nh)]
        for s in range(ck):
            w = q * k[s:s + 1, :] * jnp.exp(jnp.where(row >= s, b - b[s:s + 1, :], -jnp.inf))
            for h in range(nh):
                hs = slice(h * HG_HEAD, (h + 1) * HG_HEAD)
                acc[h] = acc[h] + jnp.sum(w[:, hs], axis=-1, keepdims=True) * v[s:s + 1, hs]
        outs = []
        for h in range(nh):
            hs = slice(h * HG_HEAD, (h + 1) * HG_HEAD)
            st = st_ref[h]
            o = acc[h] + _dotb(qe[:, hs], st, NT)
            st_ref[h] = st * eend[:, hs] + _dotb(v[:, hs], kd[:, hs], TN)
            outs.append(o * lax.rsqrt(jnp.mean(o * o, axis=-1, keepdims=True) + NORM_EPS))
        o = jnp.concatenate(outs, axis=1) * nw_ref[...] * _silu(g_ref[pl.ds(r0, ck), :])
        o_ref[pl.ds(r0, ck), :] = o.astype(BF16)
        return carry

    lax.fori_loop(0, tl // ck, chunk, 0)


def _hgrn2(proj, bsz, seq, lower_bound, norm_w):
    nl = seq // TIME_BLOCK
    return pl.pallas_call(
        _hgrn2_body,
        grid=(bsz, nl),
        in_specs=[
            _col_spec(GW, C_HG_Q, nl), _col_spec(GW, C_HG_F, nl), _col_spec(GW, C_HG_I, nl), _col_spec(GW, C_HG_G, nl),
            _row_spec(GW), _row_spec(GW),
        ],
        out_specs=pl.BlockSpec((TIME_BLOCK, GW), lambda b, l: (b * nl + l, 0)),
        out_shape=jax.ShapeDtypeStruct((bsz * seq, GW), BF16),
        scratch_shapes=[pltpu.VMEM((GW // HG_HEAD, HG_HEAD, HG_HEAD), F32)],
        compiler_params=_cparams(("arbitrary", "arbitrary")),
        name="hgrn2",
    )(proj, proj, proj, proj, lower_bound.reshape(1, GW), norm_w.reshape(1, GW))


RW_T = 32
RW_HP = 4
RW_GL = RW_HP * RW_HEAD_DIM
RW_NG = RW_HEADS // RW_HP


def _rwkv7_body(r_ref, k_ref, v_ref, wd_ref, ad_ref, gd_ref, mur_ref, muk_ref, muv_ref, muwd_ref, muad_ref, mugd_ref,
                w0_ref, w2_ref, a0_ref, a2_ref, g2_ref, kkw_ref, kaw_ref, rkw_ref, lnw_ref, lnb_ref,
                o_ref, car_r, car_k, car_v, car_wd, car_ad, car_gd, st_ref,
                kap_s, bet_s, kh_s, rho_s, v_s, c_s, y_s,
                kaps_u, rhos_u, bete_u, inv_u, arb_u, q1_u, q2_u, vk_u):
    tl, ct = TIME_BLOCK, RW_T
    srows = RW_HP * ct

    @pl.when(pl.program_id(1) == 0)
    def _():
        for car in (car_r, car_k, car_v, car_wd, car_ad, car_gd):
            car[...] = jnp.zeros_like(car)
        st_ref[...] = jnp.zeros_like(st_ref)

    def tshift(x_ref, car, mu_ref):
        x = x_ref[...]
        row = _iota2(x.shape, 0)
        prev = jnp.where(row == 0, car[0:1, :], pltpu.roll(x, 1, 0))
        car[0:1, :] = x[tl - 1:tl, :]
        return x + (prev - x) * mu_ref[...]

    r = tshift(r_ref, car_r, mur_ref)
    k = tshift(k_ref, car_k, muk_ref)
    v = tshift(v_ref, car_v, muv_ref)
    wd = tshift(wd_ref, car_wd, muwd_ref)
    ad = tshift(ad_ref, car_ad, muad_ref)
    gd = tshift(gd_ref, car_gd, mugd_ref)

    lw = -jnp.exp(-_softplus(-(w0_ref[...] + _dotb(jnp.tanh(wd), w2_ref[...]))) - 0.5)
    a = _sigmoid(a0_ref[...] + _dotb(ad, a2_ref[...]))
    g = _dotb(_sigmoid(gd), g2_ref[...])
    seg = (_iota2((GW, GW), 0) // RW_HEAD_DIM == _iota2((GW, GW), 1) // RW_HEAD_DIM).astype(BF16)
    kkr = k * kkw_ref[...]
    kk = kkr / jnp.maximum(jnp.sqrt(_dot_exact_rhs(kkr * kkr, seg)), 1e-12)
    k2 = k * (1.0 + (a - 1.0) * kaw_ref[...])
    blocktri = ((_iota2((tl, tl), 0) >= _iota2((tl, tl), 1))
                & (_iota2((tl, tl), 0) // ct == _iota2((tl, tl), 1) // ct)).astype(BF16)
    c = _dot_exact_lhs(blocktri, lw)
    enc = jnp.exp(-c)
    kap_s[...] = kk * jnp.exp(c - lw)
    bet_s[...] = kk * a * enc
    kh_s[...] = k2 * enc
    rho_s[...] = r * jnp.exp(c)
    v_s[...] = v
    c_s[...] = c

    rs = _iota2((srows, RW_GL), 0)
    hmask = (rs // ct) == (_iota2((srows, RW_GL), 1) // RW_HEAD_DIM)
    ri, cj = _iota2((srows, srows), 0), _iota2((srows, srows), 1)
    same = (ri // ct) == (cj // ct)
    strict = same & (ri > cj)
    incl = same & (ri >= cj)
    eye = (ri == cj).astype(F32)

    def stack(x):
        return jnp.where(hmask, jnp.concatenate([x] * RW_HP, axis=0), 0.0)

    nck = tl // ct
    bnt = (((2,), (2,)), ((0,), (0,)))
    bnn = (((2,), (1,)), ((0,), (0,)))
    btn = (((1,), (1,)), ((0,), (0,)))

    def bdot(a_, b_, dims):
        return lax.dot_general(a_.astype(BF16), b_.astype(BF16), dims, preferred_element_type=F32)

    def stack3(x):
        return jnp.where(hmask[None], jnp.concatenate([x] * RW_HP, axis=1), 0.0).astype(BF16)

    for gi in range(RW_NG):
        us = slice(gi * nck, (gi + 1) * nck)
        gl = slice(gi * RW_GL, (gi + 1) * RW_GL)
        to3 = lambda ref: ref[:, gl].reshape(nck, ct, RW_GL)
        eup = jnp.exp(to3(c_s)[:, ct - 1:ct, :])
        bet, kh = to3(bet_s), to3(kh_s)
        kaps, bets, khs = stack3(to3(kap_s)), stack3(bet), stack3(kh)
        rhos, vs = stack3(to3(rho_s)), stack3(to3(v_s))
        bet_e, kh_e = stack3(bet * eup), stack3(kh * eup)
        a_ab = jnp.where(strict[None], bdot(kaps, bets, bnt), 0.0)
        a_ak = jnp.where(strict[None], bdot(kaps, khs, bnt), 0.0)
        a_rb = jnp.where(incl[None], bdot(rhos, bets, bnt), 0.0)
        a_rk = jnp.where(incl[None], bdot(rhos, khs, bnt), 0.0)
        pw = -a_ab
        inv = eye[None] + pw
        for _ in range(int(math.log2(ct)) - 1):
            pw = bdot(pw, pw, bnn)
            inv = inv + bdot(inv, pw, bnn)
        kaps_u[us] = kaps
        rhos_u[us] = rhos
        bete_u[us] = bet_e
        inv_u[us] = inv.astype(BF16)
        arb_u[us] = a_rb.astype(BF16)
        q1_u[us] = bdot(a_ak, vs, bnn)
        q2_u[us] = bdot(a_rk, vs, bnn)
        vk_u[us] = bdot(vs, kh_e, btn)

    def chunk(ci, carry):
        r0 = pl.multiple_of(ci * ct, ct)
        for gi in range(RW_NG):
            un = gi * nck + ci
            gl = slice(gi * RW_GL, (gi + 1) * RW_GL)
            eup = jnp.exp(c_s[pl.ds(r0 + ct - 1, 1), gl])
            s0 = st_ref[gi]
            s16 = s0.astype(BF16)
            u = -_dotb(inv_u[un], _dotb(kaps_u[un], s16, NT) + q1_u[un])
            ys = _dotb(rhos_u[un], s16, NT) + _dotb(arb_u[un], u) + q2_u[un]
            y = ys[0:ct]
            for hh in range(1, RW_HP):
                y = y + ys[hh * ct:(hh + 1) * ct]
            y_s[pl.ds(r0, ct), gl] = y
            st_ref[gi] = s0 * eup + _dotb(u, bete_u[un], TN) + vk_u[un]
        return carry

    lax.fori_loop(0, tl // ct, chunk, 0)

    y = y_s[...]
    inv_n = 1.0 / RW_HEAD_DIM
    mean = _dot_exact_rhs(y, seg) * inv_n
    yc = y - mean
    var = _dot_exact_rhs(yc * yc, seg) * inv_n
    yn = yc * lax.rsqrt(var + RW_GN_EPS) * lnw_ref[...] + lnb_ref[...]
    bonus = _dot_exact_rhs(r * k2 * rkw_ref[...], seg) * v
    o_ref[...] = ((yn + bonus) * g).astype(BF16)


def _rwkv7(proj, bsz, seq, mu, w0, w2, a0, a2, g2, k_k, k_a, r_k, ln_w, ln_b):
    nl = seq // TIME_BLOCK
    row = lambda v_: v_.reshape(1, -1)
    padl = lambda v_: jnp.pad(v_, (0, LANES - v_.shape[0])).reshape(1, LANES)
    padr = lambda m: jnp.pad(m, ((0, LANES - m.shape[0]), (0, 0))).astype(BF16)
    o = 3 * GW
    mus = [row(mu[:GW]), row(mu[GW:2 * GW]), row(mu[2 * GW:o]), padl(mu[o:o + RW_LORA]),
           padl(mu[o + RW_LORA:o + 2 * RW_LORA]), row(mu[o + 2 * RW_LORA:])]
    f32buf = lambda w: pltpu.VMEM((TIME_BLOCK, w), F32)
    car = lambda w: pltpu.VMEM((SUBLANES, w), F32)
    units = (TIME_BLOCK // RW_T) * RW_NG
    srows = RW_HP * RW_T
    return pl.pallas_call(
        _rwkv7_body,
        grid=(bsz, nl),
        in_specs=[
            _col_spec(GW, C_RW_R, nl), _col_spec(GW, C_RW_K, nl), _col_spec(GW, C_RW_V, nl),
            _col_spec(LANES, C_RW_WD, nl), _col_spec(LANES, C_RW_AD, nl), _col_spec(RW_GATE, C_RW_GD, nl),
            _row_spec(GW), _row_spec(GW), _row_spec(GW), _row_spec(LANES), _row_spec(LANES), _row_spec(RW_GATE),
            _row_spec(GW), _full_spec((LANES, GW)), _row_spec(GW), _full_spec((LANES, GW)), _full_spec((RW_GATE, GW)),
            _row_spec(GW), _row_spec(GW), _row_spec(GW), _row_spec(GW), _row_spec(GW),
        ],
        out_specs=pl.BlockSpec((TIME_BLOCK, GW), lambda b, l: (b * nl + l, 0)),
        out_shape=jax.ShapeDtypeStruct((bsz * seq, GW), BF16),
        scratch_shapes=[car(GW), car(GW), car(GW), car(LANES), car(LANES), car(RW_GATE),
                        pltpu.VMEM((RW_NG, RW_GL, RW_GL), F32)] + [f32buf(GW)] * 7 + [
                            pltpu.VMEM((units, srows, RW_GL), BF16), pltpu.VMEM((units, srows, RW_GL), BF16),
                            pltpu.VMEM((units, srows, RW_GL), BF16), pltpu.VMEM((units, srows, srows), BF16),
                            pltpu.VMEM((units, srows, srows), BF16), pltpu.VMEM((units, srows, RW_GL), F32),
                            pltpu.VMEM((units, srows, RW_GL), F32), pltpu.VMEM((units, RW_GL, RW_GL), F32)],
        compiler_params=_cparams(("arbitrary", "arbitrary")),
        name="rwkv7",
    )(proj, proj, proj, proj, proj, proj, *mus,
      row(w0), padr(w2), row(a0), padr(a2), g2.astype(BF16), row(k_k), row(k_a), row(r_k), row(ln_w), row(ln_b))


ROUTE_TM = 256
R_E1, R_E2, R_G1, R_G2, R_K1, R_K2 = 0, 1, 2, 3, 4, 5


def _outproj_router_body(h_ref, ya_ref, yb_ref, yc_ref, yd_ref, wo_ref, nw_ref, wr_ref, br_ref,
                         ho_ref, xn_ref, route_ref, cnt_ref, cnt_s):
    tm = ROUTE_TM

    @pl.when(pl.program_id(0) == 0)
    def _():
        cnt_s[...] = jnp.zeros_like(cnt_s)

    acc = h_ref[...]
    for gi, y_ref in enumerate((ya_ref, yb_ref, yc_ref, yd_ref)):
        acc = acc + jnp.dot(y_ref[...], wo_ref[gi * GW:(gi + 1) * GW, :], preferred_element_type=F32)
    ho_ref[...] = acc
    xn = acc * lax.rsqrt(jnp.mean(acc * acc, axis=-1, keepdims=True) + NORM_EPS) * nw_ref[...]
    xn_ref[...] = xn
    logits = _dot3(xn, wr_ref[...]) + br_ref[...]

    lane = _iota2((tm, LANES), 1)
    neg = -jnp.inf
    gl = jnp.where(lane < N_GROUPS, logits, neg)
    mg = jnp.max(gl, axis=-1, keepdims=True)
    g_top = 1.0 / jnp.sum(jnp.exp(gl - mg), axis=-1, keepdims=True)
    g_idx = jnp.min(jnp.where(gl == mg, lane, LANES), axis=-1, keepdims=True)
    lo = N_GROUPS + E_PER_GROUP * g_idx
    el = jnp.where(lane >= lo, jnp.where(lane < lo + E_PER_GROUP, logits, neg), neg)
    me = jnp.max(el, axis=-1, keepdims=True)
    se = jnp.sum(jnp.exp(el - me), axis=-1, keepdims=True)
    i1 = jnp.min(jnp.where(el == me, lane, LANES), axis=-1, keepdims=True)
    el2 = jnp.where(lane == i1, neg, el)
    m2 = jnp.max(el2, axis=-1, keepdims=True)
    i2 = jnp.min(jnp.where(el2 == m2, lane, LANES), axis=-1, keepdims=True)
    p1 = 1.0 / se
    p2 = jnp.exp(m2 - me) / se
    gate1 = g_top * p1 / (p1 + p2)
    gate2 = g_top * p2 / (p1 + p2)
    e1 = i1 - N_GROUPS
    e2 = i2 - N_GROUPS

    oh1 = lane == e1
    oh2 = lane == e2
    oh = jnp.where(oh1, 1.0, jnp.where(oh2, 1.0, 0.0))
    stri = (_iota2((tm, tm), 0) > _iota2((tm, tm), 1)).astype(BF16)
    before = jnp.dot(stri, oh.astype(BF16), preferred_element_type=F32) + cnt_s[0:1, :]
    k1 = jnp.sum(jnp.where(oh1, before, 0.0), axis=-1, keepdims=True)
    k2 = jnp.sum(jnp.where(oh2, before, 0.0), axis=-1, keepdims=True)
    cnt = cnt_s[0:1, :] + jnp.sum(oh, axis=0, keepdims=True)
    cnt_s[0:1, :] = cnt
    cnt_ref[...] = jnp.broadcast_to(cnt, cnt_ref.shape)

    rec = jnp.where(lane == R_E1, e1.astype(F32), 0.0)
    rec = jnp.where(lane == R_E2, e2.astype(F32), rec)
    rec = jnp.where(lane == R_G1, gate1, rec)
    rec = jnp.where(lane == R_G2, gate2, rec)
    rec = jnp.where(lane == R_K1, k1, rec)
    rec = jnp.where(lane == R_K2, k2, rec)
    route_ref[...] = rec


def _outproj_router(h, ys, w_out16, norm_w, wr, br):
    n = h.shape[0]
    tm = ROUTE_TM
    rows = lambda w: pl.BlockSpec((tm, w), lambda i: (i, 0))
    whole = lambda shape: pl.BlockSpec(shape, lambda i: (0, 0))
    return pl.pallas_call(
        _outproj_router_body,
        grid=(n // tm,),
        in_specs=[rows(D_MODEL), rows(GW), rows(GW), rows(GW), rows(GW), whole((D_MODEL, D_MODEL)),
                  whole((1, D_MODEL)), whole((D_MODEL, LANES)), whole((1, LANES))],
        out_specs=[rows(D_MODEL), rows(D_MODEL), rows(LANES), whole((SUBLANES, LANES))],
        out_shape=[jax.ShapeDtypeStruct((n, D_MODEL), F32), jax.ShapeDtypeStruct((n, D_MODEL), F32),
                   jax.ShapeDtypeStruct((n, LANES), F32), jax.ShapeDtypeStruct((SUBLANES, LANES), F32)],
        scratch_shapes=[pltpu.VMEM((SUBLANES, LANES), F32)],
        compiler_params=_cparams(("arbitrary",)),
        name="outproj_router",
    )(h, *ys, w_out16, norm_w.reshape(1, D_MODEL), wr, br)


MOE_BLK = 256


def _moe_n_blocks(n_assign):
    return (n_assign + N_EXPERTS * (MOE_BLK - 1) + MOE_BLK - 1) // MOE_BLK


def _ffn_body(bexp_ref, stok_ref, sdst_ref, nused_ref, xn_hbm, wg_ref, wu_ref, wd_ref, y_hbm,
              xbuf, obuf, gsem, ssem):
    b = pl.program_id(0)
    nb = pl.num_programs(0)
    nused = nused_ref[0]
    blk = MOE_BLK

    def gather_start(j, slot):
        def row(r, c):
            t = stok_ref[j * blk + r]
            pltpu.make_async_copy(xn_hbm.at[pl.ds(t, 1)], xbuf.at[slot, pl.ds(r, 1)], gsem.at[slot]).start()
            return c
        lax.fori_loop(0, blk, row, 0, unroll=8)

    def gather_wait(slot):
        pltpu.make_async_copy(xn_hbm.at[pl.ds(0, blk)], xbuf.at[slot], gsem.at[slot]).wait()

    def scatter_rows(j, slot, wait):
        def row(r, c):
            a = sdst_ref[j * blk + r]

            @pl.when(a >= 0)
            def _():
                cp = pltpu.make_async_copy(obuf.at[slot, pl.ds(r, 1)], y_hbm.at[pl.ds(jnp.maximum(a, 0), 1)], ssem.at[slot])
                if wait:
                    cp.wait()
                else:
                    cp.start()
            return c
        lax.fori_loop(0, blk, row, 0, unroll=8)

    @pl.when(b == 0)
    def _():
        gather_start(0, 0)

    @pl.when(jnp.logical_and(b >= 2, b - 2 < nused))
    def _():
        scatter_rows(b - 2, b % 2, True)

    @pl.when(b < nused)
    def _():
        slot = b % 2
        gather_wait(slot)

        @pl.when(b + 1 < nused)
        def _():
            gather_start(b + 1, 1 - slot)

        x = xbuf[slot].astype(BF16)
        hid = _silu(jnp.dot(x, wg_ref[0], preferred_element_type=F32)) * jnp.dot(x, wu_ref[0], preferred_element_type=F32)
        obuf[slot] = jnp.dot(hid.astype(BF16), wd_ref[0], preferred_element_type=F32)
        scatter_rows(b, slot, False)

    @pl.when(b == nb - 1)
    def _():
        @pl.when(jnp.logical_and(b >= 1, b - 1 < nused))
        def _():
            scatter_rows(b - 1, (b - 1) % 2, True)

        @pl.when(b < nused)
        def _():
            scatter_rows(b, b % 2, True)


def _moe_ffn(xn, wg16, wu16, wd16, bexp, stok, sdst, nused):
    n = xn.shape[0]
    n_blocks = bexp.shape[0]
    wspec = lambda shape: pl.BlockSpec((1,) + shape, lambda b, be, st, sd, nu: (be[b], 0, 0))
    return pl.pallas_call(
        _ffn_body,
        grid_spec=pltpu.PrefetchScalarGridSpec(
            num_scalar_prefetch=4,
            grid=(n_blocks,),
            in_specs=[pl.BlockSpec(memory_space=pl.ANY), wspec((D_MODEL, D_EXPERT)), wspec((D_MODEL, D_EXPERT)),
                      wspec((D_EXPERT, D_MODEL))],
            out_specs=pl.BlockSpec(memory_space=pl.ANY),
            scratch_shapes=[pltpu.VMEM((2, MOE_BLK, D_MODEL), F32), pltpu.VMEM((2, MOE_BLK, D_MODEL), F32),
                            pltpu.SemaphoreType.DMA((2,)), pltpu.SemaphoreType.DMA((2,))],
        ),
        out_shape=jax.ShapeDtypeStruct((2 * n, D_MODEL), F32),
        compiler_params=_cparams(("arbitrary",)),
        name="moe_ffn",
    )(bexp, stok, sdst, nused, xn, wg16, wu16, wd16)


def _dispatch_plan(route, counts, n_blocks):
    n = route.shape[0]
    eid = route[:, R_E1:R_E2 + 1].astype(jnp.int32)
    rank = route[:, R_K1:R_K2 + 1].astype(jnp.int32)
    cnt = counts[0, :N_EXPERTS].astype(jnp.int32)
    padded = (cnt + MOE_BLK - 1) // MOE_BLK * MOE_BLK
    pad_end = jnp.cumsum(padded)
    pad_start = pad_end - padded
    onehot = eid[:, :, None] == jnp.arange(N_EXPERTS, dtype=jnp.int32)
    slot = (jnp.sum(jnp.where(onehot, pad_start, 0), axis=-1) + rank).reshape(-1)
    n_slots = n_blocks * MOE_BLK
    asg = jnp.full((n_slots,), -1, jnp.int32).at[slot].set(jnp.arange(2 * n, dtype=jnp.int32))
    stok = jnp.maximum(asg, 0) // 2
    sdst = jnp.where(asg >= 0, (asg % 2) * n + stok, -1)
    starts = jnp.arange(n_blocks, dtype=jnp.int32) * MOE_BLK
    bexp = jnp.minimum(jnp.sum(pad_end[None, :] <= starts[:, None], axis=1), N_EXPERTS - 1)
    nused = (pad_end[-1] // MOE_BLK).reshape(1).astype(jnp.int32)
    return bexp.astype(jnp.int32), stok, sdst, nused


def _combine_body(h_ref, y1_ref, y2_ref, route_ref, nw_ref, o_ref, *, final):
    rt = route_ref[...]
    out = h_ref[...] + rt[:, R_G1:R_G1 + 1] * y1_ref[...] + rt[:, R_G2:R_G2 + 1] * y2_ref[...]
    if final:
        out = out * lax.rsqrt(jnp.mean(out * out, axis=-1, keepdims=True) + NORM_EPS) * nw_ref[...]
    o_ref[...] = out


def _combine(h, y2, route, norm_w, final, tm=256):
    n = h.shape[0]
    return pl.pallas_call(
        functools.partial(_combine_body, final=final),
        grid=(n // tm,),
        in_specs=[pl.BlockSpec((tm, D_MODEL), lambda i: (i, 0)), pl.BlockSpec((tm, D_MODEL), lambda i: (i, 0)),
                  pl.BlockSpec((tm, D_MODEL), lambda i: (i + n // tm, 0)),
                  pl.BlockSpec((tm, LANES), lambda i: (i, 0)), pl.BlockSpec((1, D_MODEL), lambda i: (0, 0))],
        out_specs=pl.BlockSpec((tm, D_MODEL), lambda i: (i, 0)),
        out_shape=jax.ShapeDtypeStruct((n, D_MODEL), F32),
        compiler_params=_cparams(("arbitrary",)),
        name="combine",
    )(h, y2, y2, route, norm_w.reshape(1, D_MODEL))


def kernel(x, w_in, w_out, norm_mix_w, norm_ffn_w, final_norm_w, hg_lb_param, hg_norm_w, ssm_conv_w, ssm_conv_b, ssm_dt_bias, ssm_a_log, ssm_d, ssm_norm_w, rg_conv_w, rg_conv_b, rg_w_a, rg_b_a, rg_w_x, rg_b_x, rg_lambda, rw_mu, rw_w0, rw_w2, rw_a0, rw_a2, rw_g2, rw_k_k, rw_k_a, rw_r_k, rw_ln_w, rw_ln_b, router_group_w, router_group_b, router_expert_w, router_expert_b, moe_w_gate, moe_w_up, moe_w_down):
    bsz, seq, d = x.shape
    n = bsz * seq
    n_blocks = _moe_n_blocks(2 * n)
    sm = jax.nn.softmax(hg_lb_param.astype(F32), axis=0)
    lower_bounds = jnp.cumsum(sm, axis=0) - sm[0]
    h = x.reshape(n, d)
    for l in range(w_in.shape[0]):
        proj = _inproj(h, norm_mix_w[l], _relayout_w_in(w_in[l]))
        ya = _hgrn2(proj, bsz, seq, lower_bounds[l], hg_norm_w[l])
        yb = _ssd(proj, bsz, seq, ssm_conv_w[l], ssm_conv_b[l], ssm_dt_bias[l], ssm_a_log[l], ssm_d[l], ssm_norm_w[l])
        yc = _rglru(proj, bsz, seq, rg_conv_w[l], rg_conv_b[l], rg_w_a[l], rg_b_a[l], rg_w_x[l], rg_b_x[l], rg_lambda[l])
        yd = _rwkv7(proj, bsz, seq, rw_mu[l], rw_w0[l], rw_w2[l], rw_a0[l], rw_a2[l], rw_g2[l], rw_k_k[l], rw_k_a[l],
                    rw_r_k[l], rw_ln_w[l], rw_ln_b[l])
        wr = jnp.pad(jnp.concatenate([router_group_w[l], router_expert_w[l]], axis=1),
                     ((0, 0), (0, LANES - N_GROUPS - N_EXPERTS)))
        br = jnp.pad(jnp.concatenate([router_group_b[l], router_expert_b[l]]), (0, LANES - N_GROUPS - N_EXPERTS)).reshape(1, LANES)
        h, xn, route, counts = _outproj_router(h, (ya, yb, yc, yd), w_out[l].astype(BF16), norm_ffn_w[l], wr, br)
        bexp, stok, sdst, nused = _dispatch_plan(route, counts, n_blocks)
        y2 = _moe_ffn(xn, moe_w_gate[l].astype(BF16), moe_w_up[l].astype(BF16), moe_w_down[l].astype(BF16),
                      bexp, stok, sdst, nused)
        h = _combine(h, y2, route, final_norm_w, final=(l == w_in.shape[0] - 1))
    return h.reshape(bsz, seq, d)
```

```python
import functools
import math

import jax
import jax.numpy as jnp
from jax import lax
from jax.experimental import pallas as pl
from jax.experimental.pallas import tpu as pltpu

F32 = jnp.float32
BF16 = jnp.bfloat16

D_MODEL = 2048
GW = 512
NORM_EPS = 1e-6
HG_HEAD = 128
SSM_HEADS = 8
SSM_HEAD_DIM = 64
SSM_STATE = 128
SSM_GROUPS = 2
RG_C = 8.0
RW_HEADS = 8
RW_HEAD_DIM = 64
RW_LORA = 96
RW_GATE = 256
RW_GN_EPS = 64e-5
N_GROUPS = 4
E_PER_GROUP = 8
N_EXPERTS = 32
D_EXPERT = 1024

LANES = 128
SUBLANES = 8
VMEM_LIMIT = 56 * 1024 * 1024

C_HG_Q, C_HG_F, C_HG_I, C_HG_G = 0, 512, 1024, 1536
C_SSM_Z, C_SSM_X, C_SSM_B, C_SSM_C = 2048, 2560, 3072, 3328
C_RG_GATE, C_RG_X = 3584, 4096
C_RW_R, C_RW_K, C_RW_V = 4608, 5120, 5632
C_RW_WD, C_RW_AD, C_RW_GD = 6144, 6272, 6400
C_SSM_DT = 6656
IN_COLS_PAD = 6912

TIME_BLOCK = 256


def _cparams(sem):
    return pltpu.CompilerParams(dimension_semantics=sem, vmem_limit_bytes=VMEM_LIMIT)


def _split3(x):
    hi = x.astype(BF16)
    r1 = x - hi.astype(F32)
    mid = r1.astype(BF16)
    lo = (r1 - mid.astype(F32)).astype(BF16)
    return hi, mid, lo


def _dot_exact_rhs(x, m_bf16, dims=(((1,), (0,)), ((), ()))):
    hi, mid, lo = _split3(x)
    f = lambda a: lax.dot_general(a, m_bf16, dims, preferred_element_type=F32)
    return f(hi) + f(mid) + f(lo)


def _dot_exact_lhs(m_bf16, x, dims=(((1,), (0,)), ((), ()))):
    hi, mid, lo = _split3(x)
    f = lambda a: lax.dot_general(m_bf16, a, dims, preferred_element_type=F32)
    return f(hi) + f(mid) + f(lo)


def _dot3(a, b, dims=(((1,), (0,)), ((), ()))):
    ah = a.astype(BF16)
    al = (a - ah.astype(F32)).astype(BF16)
    bh = b.astype(BF16)
    bl = (b - bh.astype(F32)).astype(BF16)
    f = lambda p, q: lax.dot_general(p, q, dims, preferred_element_type=F32)
    return f(ah, bh) + f(ah, bl) + f(al, bh)


def _dotb(a, b, dims=(((1,), (0,)), ((), ()))):
    return lax.dot_general(a.astype(BF16), b.astype(BF16), dims, preferred_element_type=F32)


NT = (((1,), (1,)), ((), ()))
TN = (((0,), (0,)), ((), ()))


def _sigmoid(x):
    return 1.0 / (1.0 + jnp.exp(-x))


def _silu(x):
    return x * _sigmoid(x)


def _softplus(x):
    return jnp.maximum(x, 0.0) + jnp.log1p(jnp.exp(-jnp.abs(x)))


def _expm1(x):
    u = jnp.exp(x)
    return jnp.where(u == 1.0, x, jnp.where(u == 0.0, -1.0, (u - 1.0) * x / jnp.log(u)))


def _iota2(shape, axis):
    return lax.broadcasted_iota(jnp.int32, shape, axis)


def _inproj_body(x_ref, nw_ref, w_ref, o_ref, u_ref):
    @pl.when(pl.program_id(1) == 0)
    def _():
        x = x_ref[...]
        ms = jnp.mean(x * x, axis=-1, keepdims=True)
        u_ref[...] = (x * lax.rsqrt(ms + NORM_EPS) * nw_ref[...]).astype(BF16)

    o_ref[...] = jnp.dot(u_ref[...], w_ref[...], preferred_element_type=F32)


def _inproj(h, norm_w, w_pad, tm=1024, tn=768):
    n = h.shape[0]
    return pl.pallas_call(
        _inproj_body,
        grid=(n // tm, IN_COLS_PAD // tn),
        in_specs=[
            pl.BlockSpec((tm, D_MODEL), lambda i, j: (i, 0)),
            pl.BlockSpec((1, D_MODEL), lambda i, j: (0, 0)),
            pl.BlockSpec((D_MODEL, tn), lambda i, j: (0, j)),
        ],
        out_specs=pl.BlockSpec((tm, tn), lambda i, j: (i, j)),
        out_shape=jax.ShapeDtypeStruct((n, IN_COLS_PAD), F32),
        scratch_shapes=[pltpu.VMEM((tm, D_MODEL), BF16)],
        compiler_params=_cparams(("arbitrary", "arbitrary")),
        name="inproj",
    )(h, norm_w.reshape(1, D_MODEL), w_pad)


def _relayout_w_in(w):
    z = lambda c: jnp.zeros((w.shape[0], c), w.dtype)
    parts = [
        w[:, :3584],
        w[:, 3592:4616],
        w[:, 4616:6152],
        w[:, 6152:6248], z(32),
        w[:, 6248:6344], z(32),
        w[:, 6344:6600],
        w[:, 3584:3592], z(120),
        z(IN_COLS_PAD - 6784),
    ]
    return jnp.concatenate(parts, axis=1).astype(BF16)


def _col_spec(width, col, nl):
    blk = col // width
    assert blk * width == col
    return pl.BlockSpec((TIME_BLOCK, width), lambda b, l: (b * nl + l, blk))


def _row_spec(width):
    return pl.BlockSpec((1, width), lambda b, l: (0, 0))


def _full_spec(shape):
    return pl.BlockSpec(shape, lambda b, l: tuple(0 for _ in shape))


def _causal_conv4(buf_ref, x, w_ref, b_ref):
    tl = x.shape[0]
    buf_ref[pl.ds(SUBLANES, tl), :] = x
    y = b_ref[...] + w_ref[3:4, :] * x
    for j in range(3):
        y = y + w_ref[j:j + 1, :] * buf_ref[pl.ds(SUBLANES - 3 + j, tl), :]
    buf_ref[pl.ds(0, SUBLANES), :] = buf_ref[pl.ds(tl, SUBLANES), :]
    return y


def _rglru_body(gate_ref, x_ref, cw_ref, cb_ref, wa_ref, ba_ref, wx_ref, bx_ref, lam_ref,
                o_ref, xbuf, hcar):
    tl = TIME_BLOCK

    @pl.when(pl.program_id(1) == 0)
    def _():
        xbuf[pl.ds(0, SUBLANES), :] = jnp.zeros((SUBLANES, GW), F32)
        hcar[...] = jnp.zeros_like(hcar)

    xb = _causal_conv4(xbuf, x_ref[...], cw_ref, cb_ref)
    xb16 = xb.astype(BF16)
    r = _sigmoid(jnp.dot(xb16, wa_ref[...], preferred_element_type=F32) + ba_ref[...])
    i = _sigmoid(jnp.dot(xb16, wx_ref[...], preferred_element_type=F32) + bx_ref[...])
    log_a = -RG_C * r * _softplus(-lam_ref[...])
    a = jnp.exp(log_a)
    u = jnp.sqrt(-_expm1(2.0 * log_a)) * (i * xb)
    row = _iota2((tl, GW), 0)
    d = 1
    while d < tl:
        keep = row >= d
        a_s = jnp.where(keep, pltpu.roll(a, d, 0), 1.0)
        u_s = jnp.where(keep, pltpu.roll(u, d, 0), 0.0)
        u = a * u_s + u
        a = a * a_s
        d *= 2
    h = a * hcar[0:1, :] + u
    hcar[0:1, :] = h[tl - 1:tl, :]
    o_ref[...] = (h * jax.nn.gelu(gate_ref[...], approximate=True)).astype(BF16)


def _block_diag(w):
    nb, k, _ = w.shape
    eye = jnp.eye(nb, dtype=w.dtype)
    return (eye[:, None, :, None] * w[:, :, None, :]).reshape(nb * k, nb * k)


def _rglru(proj, bsz, seq, cw, cb, w_a, b_a, w_x, b_x, lam):
    nl = seq // TIME_BLOCK
    r = lambda v: v.reshape(1, GW)
    return pl.pallas_call(
        _rglru_body,
        grid=(bsz, nl),
        in_specs=[
            _col_spec(GW, C_RG_GATE, nl), _col_spec(GW, C_RG_X, nl),
            _full_spec((4, GW)), _row_spec(GW),
            _full_spec((GW, GW)), _row_spec(GW), _full_spec((GW, GW)), _row_spec(GW), _row_spec(GW),
        ],
        out_specs=pl.BlockSpec((TIME_BLOCK, GW), lambda b, l: (b * nl + l, 0)),
        out_shape=jax.ShapeDtypeStruct((bsz * seq, GW), BF16),
        scratch_shapes=[pltpu.VMEM((TIME_BLOCK + SUBLANES, GW), F32), pltpu.VMEM((SUBLANES, GW), F32)],
        compiler_params=_cparams(("arbitrary", "arbitrary")),
        name="rglru",
    )(proj, proj, cw, r(cb), _block_diag(w_a).astype(BF16), r(b_a), _block_diag(w_x).astype(BF16), r(b_x), r(lam))


SSD_CHUNK = 64


def _ssd_body(z_ref, x_ref, b_ref, c_ref, dt_ref, cwx_ref, cwb_ref, cwc_ref, cbx_ref, cbb_ref, cbc_ref,
              dtb_ref, alog_ref, dsk_ref, nw_ref, o_ref, xbuf, bbuf, cbuf, st_ref):
    tl, ck = TIME_BLOCK, SSD_CHUNK
    npair = SSM_HEADS // 2

    @pl.when(pl.program_id(1) == 0)
    def _():
        xbuf[pl.ds(0, SUBLANES), :] = jnp.zeros((SUBLANES, GW), F32)
        bbuf[pl.ds(0, SUBLANES), :] = jnp.zeros((SUBLANES, 2 * SSM_STATE), F32)
        cbuf[pl.ds(0, SUBLANES), :] = jnp.zeros((SUBLANES, 2 * SSM_STATE), F32)
        st_ref[...] = jnp.zeros_like(st_ref)

    xs_all = _silu(_causal_conv4(xbuf, x_ref[...], cwx_ref, cbx_ref))
    bm_all = _silu(_causal_conv4(bbuf, b_ref[...], cwb_ref, cbb_ref))
    cm_all = _silu(_causal_conv4(cbuf, c_ref[...], cwc_ref, cbc_ref))
    dt_all = _softplus(dt_ref[...] + dtb_ref[...])
    a_all = dt_all * (-jnp.exp(alog_ref[...]))

    expand = (_iota2((LANES, GW), 1) // SSM_HEAD_DIM == _iota2((LANES, GW), 0)).astype(BF16)
    tri = (_iota2((ck, ck), 0) >= _iota2((ck, ck), 1))
    tri16 = tri.astype(BF16)
    lane = _iota2((ck, LANES), 1)
    outs = []
    for c in range(tl // ck):
        sl = slice(c * ck, (c + 1) * ck)
        xs, bm, cm, dt, a = xs_all[sl], bm_all[sl], cm_all[sl], dt_all[sl], a_all[sl]
        acum = _dot_exact_lhs(tri16, a)
        acum_t = jnp.transpose(acum)
        acum_x = _dot_exact_rhs(acum, expand)
        dt_x = _dot_exact_rhs(dt, expand)
        aend_x = acum_x[ck - 1:ck, :]
        xdt = xs * dt_x
        xdec = xdt * jnp.exp(aend_x - acum_x)
        eacum = jnp.exp(acum_x)
        eend = jnp.exp(aend_x)
        ys = []
        for p in range(npair):
            g = (2 * p) // (SSM_HEADS // SSM_GROUPS)
            gs = slice(g * SSM_STATE, (g + 1) * SSM_STATE)
            ps = slice(p * LANES, (p + 1) * LANES)
            gmat = _dotb(cm[:, gs], bm[:, gs], NT)
            yd = []
            for hh in range(2):
                h = 2 * p + hh
                rel = acum[:, h:h + 1] - acum_t[h:h + 1, :]
                dec = jnp.exp(jnp.where(tri, rel, -jnp.inf))
                yd.append(_dotb(gmat * dec, xdt[:, ps]))
            y_diag = jnp.where(lane < SSM_HEAD_DIM, yd[0], yd[1])
            st = st_ref[p]
            y_off = _dotb(cm[:, gs], st) * eacum[:, ps]
            st_ref[p] = st * eend[:, ps] + _dotb(bm[:, gs], xdec[:, ps], TN)
            ys.append(y_diag + y_off)
        y = jnp.concatenate(ys, axis=1) + xs * dsk_ref[...]
        outs.append(y)
    y = jnp.concatenate(outs, axis=0) * _silu(z_ref[...])
    half = GW // SSM_GROUPS
    parts = []
    for g in range(SSM_GROUPS):
        yg = y[:, g * half:(g + 1) * half]
        parts.append(yg * lax.rsqrt(jnp.mean(yg * yg, axis=-1, keepdims=True) + NORM_EPS))
    o_ref[...] = (jnp.concatenate(parts, axis=1) * nw_ref[...]).astype(BF16)


def _ssd(proj, bsz, seq, conv_w, conv_b, dt_bias, a_log, d_skip, norm_w):
    nl = seq // TIME_BLOCK
    ns2 = 2 * SSM_STATE
    pad_row = lambda v: jnp.pad(v, (0, LANES - v.shape[0])).reshape(1, LANES)
    return pl.pallas_call(
        _ssd_body,
        grid=(bsz, nl),
        in_specs=[
            _col_spec(GW, C_SSM_Z, nl), _col_spec(GW, C_SSM_X, nl), _col_spec(ns2, C_SSM_B, nl),
            _col_spec(ns2, C_SSM_C, nl), _col_spec(LANES, C_SSM_DT, nl),
            _full_spec((4, GW)), _full_spec((4, ns2)), _full_spec((4, ns2)),
            _row_spec(GW), _row_spec(ns2), _row_spec(ns2),
            _row_spec(LANES), _row_spec(LANES), _row_spec(GW), _row_spec(GW),
        ],
        out_specs=pl.BlockSpec((TIME_BLOCK, GW), lambda b, l: (b * nl + l, 0)),
        out_shape=jax.ShapeDtypeStruct((bsz * seq, GW), BF16),
        scratch_shapes=[
            pltpu.VMEM((TIME_BLOCK + SUBLANES, GW), F32),
            pltpu.VMEM((TIME_BLOCK + SUBLANES, ns2), F32),
            pltpu.VMEM((TIME_BLOCK + SUBLANES, ns2), F32),
            pltpu.VMEM((SSM_HEADS // 2, SSM_STATE, LANES), F32),
        ],
        compiler_params=_cparams(("arbitrary", "arbitrary")),
        name="ssd",
    )(proj, proj, proj, proj, proj,
      conv_w[:, :GW], conv_w[:, GW:GW + ns2], conv_w[:, GW + ns2:],
      conv_b[:GW].reshape(1, GW), conv_b[GW:GW + ns2].reshape(1, ns2), conv_b[GW + ns2:].reshape(1, ns2),
      pad_row(dt_bias), pad_row(a_log), jnp.repeat(d_skip, SSM_HEAD_DIM).reshape(1, GW), norm_w.reshape(1, GW))


HG_CHUNK = 16


def _hgrn2_body(q_ref, f_ref, i_ref, g_ref, lb_ref, nw_ref, o_ref, st_ref):
    tl, ck = TIME_BLOCK, HG_CHUNK
    nh = GW // HG_HEAD

    @pl.when(pl.program_id(1) == 0)
    def _():
        st_ref[...] = jnp.zeros_like(st_ref)

    lb = lb_ref[...]
    row = _iota2((ck, GW), 0)
    tri16 = (_iota2((ck, ck), 0) >= _iota2((ck, ck), 1)).astype(BF16)

    def chunk(c, carry):
        r0 = pl.multiple_of(c * ck, ck)
        q = _silu(q_ref[pl.ds(r0, ck), :])
        f = lb + (1.0 - lb) * _sigmoid(f_ref[pl.ds(r0, ck), :])
        k = 1.0 - f
        v = i_ref[pl.ds(r0, ck), :]
        b = _dot_exact_lhs(tri16, jnp.log(f))
        bend = b[ck - 1:ck, :]
        qe = q * jnp.exp(b)
        kd = k * jnp.exp(bend - b)
        eend = jnp.exp(bend)
        acc = [jnp.zeros((ck, HG_HEAD), F32) for _ in range(nh)]
        for s in range(ck):
            w = q * k[s:s + 1, :] * jnp.exp(jnp.where(row >= s, b - b[s:s + 1, :], -jnp.inf))
            for h in range(nh):
                hs = slice(h * HG_HEAD, (h + 1) * HG_HEAD)
                acc[h] = acc[h] + jnp.sum(w[:, hs], axis=-1, keepdims=True) * v[s:s + 1, hs]
        outs = []
        for h in range(nh):
            hs = slice(h * HG_HEAD, (h + 1) * HG_HEAD)
            st = st_ref[h]
            o = acc[h] + _dotb(qe[:, hs], st, NT)
            st_ref[h] = st * eend[:, hs] + _dotb(v[:, hs], kd[:, hs], TN)
            outs.append(o * lax.rsqrt(jnp.mean(o * o, axis=-1, keepdims=True) + NORM_EPS))
        o = jnp.concatenate(outs, axis=1) * nw_ref[...] * _silu(g_ref[pl.ds(r0, ck), :])
        o_ref[pl.ds(r0, ck), :] = o.astype(BF16)
        return carry

    lax.fori_loop(0, tl // ck, chunk, 0, unroll=8)


def _hgrn2(proj, bsz, seq, lower_bound, norm_w):
    nl = seq // TIME_BLOCK
    return pl.pallas_call(
        _hgrn2_body,
        grid=(bsz, nl),
        in_specs=[
            _col_spec(GW, C_HG_Q, nl), _col_spec(GW, C_HG_F, nl), _col_spec(GW, C_HG_I, nl), _col_spec(GW, C_HG_G, nl),
            _row_spec(GW), _row_spec(GW),
        ],
        out_specs=pl.BlockSpec((TIME_BLOCK, GW), lambda b, l: (b * nl + l, 0)),
        out_shape=jax.ShapeDtypeStruct((bsz * seq, GW), BF16),
        scratch_shapes=[pltpu.VMEM((GW // HG_HEAD, HG_HEAD, HG_HEAD), F32)],
        compiler_params=_cparams(("arbitrary", "arbitrary")),
        name="hgrn2",
    )(proj, proj, proj, proj, lower_bound.reshape(1, GW), norm_w.reshape(1, GW))


RW_T = 32
RW_HP = 4
RW_GL = RW_HP * RW_HEAD_DIM
RW_NG = RW_HEADS // RW_HP


def _rwkv7_body(r_ref, k_ref, v_ref, wd_ref, ad_ref, gd_ref, mur_ref, muk_ref, muv_ref, muwd_ref, muad_ref, mugd_ref,
                w0_ref, w2_ref, a0_ref, a2_ref, g2_ref, kkw_ref, kaw_ref, rkw_ref, lnw_ref, lnb_ref,
                o_ref, car_r, car_k, car_v, car_wd, car_ad, car_gd, st_ref,
                kap_s, bet_s, kh_s, rho_s, v_s, c_s, y_s,
                r1_u, r2_u, mm_u, cc_u):
    tl, ct = TIME_BLOCK, RW_T
    srows = RW_HP * ct

    @pl.when(pl.program_id(1) == 0)
    def _():
        for car in (car_r, car_k, car_v, car_wd, car_ad, car_gd):
            car[...] = jnp.zeros_like(car)
        st_ref[...] = jnp.zeros_like(st_ref)

    def tshift(x_ref, car, mu_ref):
        x = x_ref[...]
        row = _iota2(x.shape, 0)
        prev = jnp.where(row == 0, car[0:1, :], pltpu.roll(x, 1, 0))
        car[0:1, :] = x[tl - 1:tl, :]
        return x + (prev - x) * mu_ref[...]

    r = tshift(r_ref, car_r, mur_ref)
    k = tshift(k_ref, car_k, muk_ref)
    v = tshift(v_ref, car_v, muv_ref)
    wd = tshift(wd_ref, car_wd, muwd_ref)
    ad = tshift(ad_ref, car_ad, muad_ref)
    gd = tshift(gd_ref, car_gd, mugd_ref)

    lw = -jnp.exp(-_softplus(-(w0_ref[...] + _dotb(jnp.tanh(wd), w2_ref[...]))) - 0.5)
    a = _sigmoid(a0_ref[...] + _dotb(ad, a2_ref[...]))
    g = _dotb(_sigmoid(gd), g2_ref[...])
    seg = (_iota2((GW, GW), 0) // RW_HEAD_DIM == _iota2((GW, GW), 1) // RW_HEAD_DIM).astype(BF16)
    kkr = k * kkw_ref[...]
    kk = kkr / jnp.maximum(jnp.sqrt(_dot_exact_rhs(kkr * kkr, seg)), 1e-12)
    k2 = k * (1.0 + (a - 1.0) * kaw_ref[...])
    blocktri = ((_iota2((tl, tl), 0) >= _iota2((tl, tl), 1))
                & (_iota2((tl, tl), 0) // ct == _iota2((tl, tl), 1) // ct)).astype(BF16)
    c = _dot_exact_lhs(blocktri, lw)
    enc = jnp.exp(-c)
    kap_s[...] = kk * jnp.exp(c - lw)
    bet_s[...] = kk * a * enc
    kh_s[...] = k2 * enc
    rho_s[...] = r * jnp.exp(c)
    v_s[...] = v
    c_s[...] = c

    rs = _iota2((srows, RW_GL), 0)
    hmask = (rs // ct) == (_iota2((srows, RW_GL), 1) // RW_HEAD_DIM)
    ri, cj = _iota2((srows, srows), 0), _iota2((srows, srows), 1)
    same = (ri // ct) == (cj // ct)
    strict = same & (ri > cj)
    incl = same & (ri >= cj)
    eye = (ri == cj).astype(F32)

    def stack(x):
        return jnp.where(hmask, jnp.concatenate([x] * RW_HP, axis=0), 0.0)

    nck = tl // ct
    bnt = (((2,), (2,)), ((0,), (0,)))
    bnn = (((2,), (1,)), ((0,), (0,)))
    btn = (((1,), (1,)), ((0,), (0,)))

    def bdot(a_, b_, dims):
        return lax.dot_general(a_.astype(BF16), b_.astype(BF16), dims, preferred_element_type=F32)

    def stack3(x):
        return jnp.where(hmask[None], jnp.concatenate([x] * RW_HP, axis=1), 0.0).astype(BF16)

    for gi in range(RW_NG):
        us = slice(gi * nck, (gi + 1) * nck)
        gl = slice(gi * RW_GL, (gi + 1) * RW_GL)
        to3 = lambda ref: ref[:, gl].reshape(nck, ct, RW_GL)
        eup = jnp.exp(to3(c_s)[:, ct - 1:ct, :])
        bet, kh = to3(bet_s), to3(kh_s)
        kaps, bets, khs = stack3(to3(kap_s)), stack3(bet), stack3(kh)
        rhos, vs = stack3(to3(rho_s)), stack3(to3(v_s))
        bet_e, kh_e = stack3(bet * eup), stack3(kh * eup)
        a_ab = jnp.where(strict[None], bdot(kaps, bets, bnt), 0.0)
        a_ak = jnp.where(strict[None], bdot(kaps, khs, bnt), 0.0)
        a_rb = jnp.where(incl[None], bdot(rhos, bets, bnt), 0.0)
        a_rk = jnp.where(incl[None], bdot(rhos, khs, bnt), 0.0)
        pw = -a_ab
        inv = eye[None] + pw
        for _ in range(int(math.log2(ct)) - 1):
            pw = bdot(pw, pw, bnn)
            inv = inv + bdot(inv, pw, bnn)
        p1 = bdot(inv, kaps, bnn)
        p2 = bdot(inv, bdot(a_ak, vs, bnn), bnn)
        r1_u[us] = (rhos.astype(F32) - bdot(a_rb, p1, bnn)).astype(BF16)
        r2_u[us] = bdot(a_rk, vs, bnn) - bdot(a_rb, p2, bnn)
        mm_u[us] = bdot(p1, bet_e, btn).astype(BF16)
        cc_u[us] = bdot(vs, kh_e, btn) - bdot(p2, bet_e, btn)

    def chunk(ci, carry):
        r0 = pl.multiple_of(ci * ct, ct)
        for gi in range(RW_NG):
            un = gi * nck + ci
            gl = slice(gi * RW_GL, (gi + 1) * RW_GL)
            eup = jnp.exp(c_s[pl.ds(r0 + ct - 1, 1), gl])
            s0 = st_ref[gi]
            s16 = s0.astype(BF16)
            ys = _dotb(r1_u[un], s16, NT) + r2_u[un]
            y = ys[0:ct]
            for hh in range(1, RW_HP):
                y = y + ys[hh * ct:(hh + 1) * ct]
            y_s[pl.ds(r0, ct), gl] = y
            st_ref[gi] = s0 * eup - _dotb(s16, mm_u[un]) + cc_u[un]
        return carry

    lax.fori_loop(0, tl // ct, chunk, 0)

    y = y_s[...]
    inv_n = 1.0 / RW_HEAD_DIM
    mean = _dot_exact_rhs(y, seg) * inv_n
    yc = y - mean
    var = _dot_exact_rhs(yc * yc, seg) * inv_n
    yn = yc * lax.rsqrt(var + RW_GN_EPS) * lnw_ref[...] + lnb_ref[...]
    bonus = _dot_exact_rhs(r * k2 * rkw_ref[...], seg) * v
    o_ref[...] = ((yn + bonus) * g).astype(BF16)


def _rwkv7(proj, bsz, seq, mu, w0, w2, a0, a2, g2, k_k, k_a, r_k, ln_w, ln_b):
    nl = seq // TIME_BLOCK
    row = lambda v_: v_.reshape(1, -1)
    padl = lambda v_: jnp.pad(v_, (0, LANES - v_.shape[0])).reshape(1, LANES)
    padr = lambda m: jnp.pad(m, ((0, LANES - m.shape[0]), (0, 0))).astype(BF16)
    o = 3 * GW
    mus = [row(mu[:GW]), row(mu[GW:2 * GW]), row(mu[2 * GW:o]), padl(mu[o:o + RW_LORA]),
           padl(mu[o + RW_LORA:o + 2 * RW_LORA]), row(mu[o + 2 * RW_LORA:])]
    f32buf = lambda w: pltpu.VMEM((TIME_BLOCK, w), F32)
    car = lambda w: pltpu.VMEM((SUBLANES, w), F32)
    units = (TIME_BLOCK // RW_T) * RW_NG
    srows = RW_HP * RW_T
    return pl.pallas_call(
        _rwkv7_body,
        grid=(bsz, nl),
        in_specs=[
            _col_spec(GW, C_RW_R, nl), _col_spec(GW, C_RW_K, nl), _col_spec(GW, C_RW_V, nl),
            _col_spec(LANES, C_RW_WD, nl), _col_spec(LANES, C_RW_AD, nl), _col_spec(RW_GATE, C_RW_GD, nl),
            _row_spec(GW), _row_spec(GW), _row_spec(GW), _row_spec(LANES), _row_spec(LANES), _row_spec(RW_GATE),
            _row_spec(GW), _full_spec((LANES, GW)), _row_spec(GW), _full_spec((LANES, GW)), _full_spec((RW_GATE, GW)),
            _row_spec(GW), _row_spec(GW), _row_spec(GW), _row_spec(GW), _row_spec(GW),
        ],
        out_specs=pl.BlockSpec((TIME_BLOCK, GW), lambda b, l: (b * nl + l, 0)),
        out_shape=jax.ShapeDtypeStruct((bsz * seq, GW), BF16),
        scratch_shapes=[car(GW), car(GW), car(GW), car(LANES), car(LANES), car(RW_GATE),
                        pltpu.VMEM((RW_NG, RW_GL, RW_GL), F32)] + [f32buf(GW)] * 7 + [
                            pltpu.VMEM((units, srows, RW_GL), BF16), pltpu.VMEM((units, srows, RW_GL), F32),
                            pltpu.VMEM((units, RW_GL, RW_GL), BF16), pltpu.VMEM((units, RW_GL, RW_GL), F32)],
        compiler_params=_cparams(("arbitrary", "arbitrary")),
        name="rwkv7",
    )(proj, proj, proj, proj, proj, proj, *mus,
      row(w0), padr(w2), row(a0), padr(a2), g2.astype(BF16), row(k_k), row(k_a), row(r_k), row(ln_w), row(ln_b))


ROUTE_TM = 256
R_E1, R_E2, R_G1, R_G2, R_K1, R_K2 = 0, 1, 2, 3, 4, 5


def _outproj_router_body(h_ref, ya_ref, yb_ref, yc_ref, yd_ref, wo_ref, nw_ref, wr_ref, br_ref,
                         ho_ref, xn_ref, route_ref, cnt_ref, cnt_s):
    tm = ROUTE_TM

    @pl.when(pl.program_id(0) == 0)
    def _():
        cnt_s[...] = jnp.zeros_like(cnt_s)

    acc = h_ref[...]
    for gi, y_ref in enumerate((ya_ref, yb_ref, yc_ref, yd_ref)):
        acc = acc + jnp.dot(y_ref[...], wo_ref[gi * GW:(gi + 1) * GW, :], preferred_element_type=F32)
    ho_ref[...] = acc
    xn = acc * lax.rsqrt(jnp.mean(acc * acc, axis=-1, keepdims=True) + NORM_EPS) * nw_ref[...]
    xn_ref[...] = xn
    logits = _dot3(xn, wr_ref[...]) + br_ref[...]

    lane = _iota2((tm, LANES), 1)
    neg = -jnp.inf
    gl = jnp.where(lane < N_GROUPS, logits, neg)
    mg = jnp.max(gl, axis=-1, keepdims=True)
    g_top = 1.0 / jnp.sum(jnp.exp(gl - mg), axis=-1, keepdims=True)
    g_idx = jnp.min(jnp.where(gl == mg, lane, LANES), axis=-1, keepdims=True)
    lo = N_GROUPS + E_PER_GROUP * g_idx
    el = jnp.where(lane >= lo, jnp.where(lane < lo + E_PER_GROUP, logits, neg), neg)
    me = jnp.max(el, axis=-1, keepdims=True)
    se = jnp.sum(jnp.exp(el - me), axis=-1, keepdims=True)
    i1 = jnp.min(jnp.where(el == me, lane, LANES), axis=-1, keepdims=True)
    el2 = jnp.where(lane == i1, neg, el)
    m2 = jnp.max(el2, axis=-1, keepdims=True)
    i2 = jnp.min(jnp.where(el2 == m2, lane, LANES), axis=-1, keepdims=True)
    p1 = 1.0 / se
    p2 = jnp.exp(m2 - me) / se
    gate1 = g_top * p1 / (p1 + p2)
    gate2 = g_top * p2 / (p1 + p2)
    e1 = i1 - N_GROUPS
    e2 = i2 - N_GROUPS

    oh1 = lane == e1
    oh2 = lane == e2
    oh = jnp.where(oh1, 1.0, jnp.where(oh2, 1.0, 0.0))
    stri = (_iota2((tm, tm), 0) > _iota2((tm, tm), 1)).astype(BF16)
    before = jnp.dot(stri, oh.astype(BF16), preferred_element_type=F32) + cnt_s[0:1, :]
    k1 = jnp.sum(jnp.where(oh1, before, 0.0), axis=-1, keepdims=True)
    k2 = jnp.sum(jnp.where(oh2, before, 0.0), axis=-1, keepdims=True)
    cnt = cnt_s[0:1, :] + jnp.sum(oh, axis=0, keepdims=True)
    cnt_s[0:1, :] = cnt
    cnt_ref[...] = jnp.broadcast_to(cnt, cnt_ref.shape)

    rec = jnp.where(lane == R_E1, e1.astype(F32), 0.0)
    rec = jnp.where(lane == R_E2, e2.astype(F32), rec)
    rec = jnp.where(lane == R_G1, gate1, rec)
    rec = jnp.where(lane == R_G2, gate2, rec)
    rec = jnp.where(lane == R_K1, k1, rec)
    rec = jnp.where(lane == R_K2, k2, rec)
    route_ref[...] = rec


def _outproj_router(h, ys, w_out16, norm_w, wr, br):
    n = h.shape[0]
    tm = ROUTE_TM
    rows = lambda w: pl.BlockSpec((tm, w), lambda i: (i, 0))
    whole = lambda shape: pl.BlockSpec(shape, lambda i: (0, 0))
    return pl.pallas_call(
        _outproj_router_body,
        grid=(n // tm,),
        in_specs=[rows(D_MODEL), rows(GW), rows(GW), rows(GW), rows(GW), whole((D_MODEL, D_MODEL)),
                  whole((1, D_MODEL)), whole((D_MODEL, LANES)), whole((1, LANES))],
        out_specs=[rows(D_MODEL), rows(D_MODEL), rows(LANES), whole((SUBLANES, LANES))],
        out_shape=[jax.ShapeDtypeStruct((n, D_MODEL), F32), jax.ShapeDtypeStruct((n, D_MODEL), F32),
                   jax.ShapeDtypeStruct((n, LANES), F32), jax.ShapeDtypeStruct((SUBLANES, LANES), F32)],
        scratch_shapes=[pltpu.VMEM((SUBLANES, LANES), F32)],
        compiler_params=_cparams(("arbitrary",)),
        name="outproj_router",
    )(h, *ys, w_out16, norm_w.reshape(1, D_MODEL), wr, br)


MOE_BLK = 256


def _moe_n_blocks(n_assign):
    return (n_assign + N_EXPERTS * (MOE_BLK - 1) + MOE_BLK - 1) // MOE_BLK


def _ffn_body(bexp_ref, stok_ref, sdst_ref, nused_ref, xn_hbm, wg_ref, wu_ref, wd_ref, y_hbm,
              xbuf, obuf, gsem, ssem):
    b = pl.program_id(0)
    last_blk = pl.num_programs(0) - 2
    nused = nused_ref[0]
    blk = MOE_BLK
    dump0 = y_hbm.shape[0] - 2 * blk
    slot = b % 2
    other = 1 - slot

    def gather_row(j, r, s):
        t = stok_ref[j * blk + r]
        pltpu.make_async_copy(xn_hbm.at[pl.ds(t, 1)], xbuf.at[s, pl.ds(r, 1)], gsem.at[s]).start()

    def scatter_row(j, r, s, all_dump):
        a = sdst_ref[j * blk + r]
        dst = jnp.where(jnp.logical_or(a < 0, all_dump), dump0 + s * blk + r, a)
        pltpu.make_async_copy(obuf.at[s, pl.ds(r, 1)], y_hbm.at[pl.ds(dst, 1)], ssem.at[s]).start()

    def gather_wait(s):
        pltpu.make_async_copy(xn_hbm.at[pl.ds(0, blk)], xbuf.at[s], gsem.at[s]).wait()

    def scatter_wait(s):
        pltpu.make_async_copy(obuf.at[s], y_hbm.at[pl.ds(0, blk)], ssem.at[s]).wait()

    @pl.when(b == 0)
    def _():
        obuf[...] = jnp.zeros_like(obuf)
        for s in range(2):
            pltpu.make_async_copy(obuf.at[s], y_hbm.at[pl.ds(dump0 + s * blk, blk)], ssem.at[s]).start()
        for s in range(2):
            pltpu.make_async_copy(obuf.at[s], y_hbm.at[pl.ds(dump0 + s * blk, blk)], ssem.at[s]).wait()

        def row(r, c):
            gather_row(0, r, 0)
            return c
        lax.fori_loop(0, blk, row, 0, unroll=8)

    @pl.when(b < nused)
    def _():
        gather_wait(slot)

        @pl.when(b >= 1)
        def _():
            scatter_wait(slot)

        nxt = jnp.minimum(b + 1, last_blk)
        prv = jnp.maximum(b - 1, 0)
        for r in range(blk):
            gather_row(nxt, r, other)
        for r in range(blk):
            scatter_row(prv, r, other, b == 0)
        x = xbuf[slot].astype(BF16)
        hid = _silu(jnp.dot(x, wg_ref[0], preferred_element_type=F32)) * jnp.dot(x, wu_ref[0], preferred_element_type=F32)
        obuf[slot] = jnp.dot(hid.astype(BF16), wd_ref[0], preferred_element_type=F32)

    @pl.when(b == nused)
    def _():
        gather_wait(slot)
        scatter_wait(slot)

        def row(r, c):
            scatter_row(b - 1, r, other, False)
            return c
        lax.fori_loop(0, blk, row, 0, unroll=8)
        scatter_wait(other)


def _moe_ffn(xn, wg16, wu16, wd16, bexp, stok, sdst, nused):
    n = xn.shape[0]
    n_steps = bexp.shape[0]
    wspec = lambda shape: pl.BlockSpec((1,) + shape, lambda b, be, st, sd, nu: (be[b], 0, 0))
    return pl.pallas_call(
        _ffn_body,
        grid_spec=pltpu.PrefetchScalarGridSpec(
            num_scalar_prefetch=4,
            grid=(n_steps,),
            in_specs=[pl.BlockSpec(memory_space=pl.ANY), wspec((D_MODEL, D_EXPERT)), wspec((D_MODEL, D_EXPERT)),
                      wspec((D_EXPERT, D_MODEL))],
            out_specs=pl.BlockSpec(memory_space=pl.ANY),
            scratch_shapes=[pltpu.VMEM((2, MOE_BLK, D_MODEL), F32), pltpu.VMEM((2, MOE_BLK, D_MODEL), F32),
                            pltpu.SemaphoreType.DMA((2,)), pltpu.SemaphoreType.DMA((2,))],
        ),
        out_shape=jax.ShapeDtypeStruct((2 * n + 2 * MOE_BLK, D_MODEL), F32),
        compiler_params=_cparams(("arbitrary",)),
        name="moe_ffn",
    )(bexp, stok, sdst, nused, xn, wg16, wu16, wd16)


def _cast_body(x_ref, o_ref):
    o_ref[...] = x_ref[...].astype(BF16)


def _expert_weights_bf16(w):
    e, a, b_ = w.shape
    spec = pl.BlockSpec((1, a, b_), lambda i: (i, 0, 0))
    return pl.pallas_call(
        _cast_body, grid=(e,), in_specs=[spec], out_specs=spec,
        out_shape=jax.ShapeDtypeStruct(w.shape, BF16),
        compiler_params=_cparams(("arbitrary",)), name="cast_bf16",
    )(w)


def _dispatch_plan(route, counts, n_blocks):
    n = route.shape[0]
    eid = route[:, R_E1:R_E2 + 1].astype(jnp.int32)
    rank = route[:, R_K1:R_K2 + 1].astype(jnp.int32)
    cnt = counts[0, :N_EXPERTS].astype(jnp.int32)
    padded = (cnt + MOE_BLK - 1) // MOE_BLK * MOE_BLK
    pad_end = jnp.cumsum(padded)
    pad_start = pad_end - padded
    onehot = eid[:, :, None] == jnp.arange(N_EXPERTS, dtype=jnp.int32)
    slot = (jnp.sum(jnp.where(onehot, pad_start, 0), axis=-1) + rank).reshape(-1)
    n_slots = n_blocks * MOE_BLK
    asg = jnp.full((n_slots,), -1, jnp.int32).at[slot].set(jnp.arange(2 * n, dtype=jnp.int32))
    stok = jnp.maximum(asg, 0) // 2
    sdst = jnp.where(asg >= 0, (asg % 2) * n + stok, -1)
    starts = jnp.arange(n_blocks + 1, dtype=jnp.int32) * MOE_BLK
    bexp = jnp.minimum(jnp.sum(pad_end[None, :] <= starts[:, None], axis=1), N_EXPERTS - 1)
    nused = (pad_end[-1] // MOE_BLK).reshape(1).astype(jnp.int32)
    return bexp.astype(jnp.int32), stok, sdst, nused


def _combine_body(h_ref, y1_ref, y2_ref, route_ref, nw_ref, o_ref, *, final):
    rt = route_ref[...]
    out = h_ref[...] + rt[:, R_G1:R_G1 + 1] * y1_ref[...] + rt[:, R_G2:R_G2 + 1] * y2_ref[...]
    if final:
        out = out * lax.rsqrt(jnp.mean(out * out, axis=-1, keepdims=True) + NORM_EPS) * nw_ref[...]
    o_ref[...] = out


def _combine(h, y2, route, norm_w, final, tm=256):
    n = h.shape[0]
    return pl.pallas_call(
        functools.partial(_combine_body, final=final),
        grid=(n // tm,),
        in_specs=[pl.BlockSpec((tm, D_MODEL), lambda i: (i, 0)), pl.BlockSpec((tm, D_MODEL), lambda i: (i, 0)),
                  pl.BlockSpec((tm, D_MODEL), lambda i: (i + n // tm, 0)),
                  pl.BlockSpec((tm, LANES), lambda i: (i, 0)), pl.BlockSpec((1, D_MODEL), lambda i: (0, 0))],
        out_specs=pl.BlockSpec((tm, D_MODEL), lambda i: (i, 0)),
        out_shape=jax.ShapeDtypeStruct((n, D_MODEL), F32),
        compiler_params=_cparams(("arbitrary",)),
        name="combine",
    )(h, y2, y2, route, norm_w.reshape(1, D_MODEL))


def kernel(x, w_in, w_out, norm_mix_w, norm_ffn_w, final_norm_w, hg_lb_param, hg_norm_w, ssm_conv_w, ssm_conv_b, ssm_dt_bias, ssm_a_log, ssm_d, ssm_norm_w, rg_conv_w, rg_conv_b, rg_w_a, rg_b_a, rg_w_x, rg_b_x, rg_lambda, rw_mu, rw_w0, rw_w2, rw_a0, rw_a2, rw_g2, rw_k_k, rw_k_a, rw_r_k, rw_ln_w, rw_ln_b, router_group_w, router_group_b, router_expert_w, router_expert_b, moe_w_gate, moe_w_up, moe_w_down):
    bsz, seq, d = x.shape
    n = bsz * seq
    n_blocks = _moe_n_blocks(2 * n)
    sm = jax.nn.softmax(hg_lb_param.astype(F32), axis=0)
    lower_bounds = jnp.cumsum(sm, axis=0) - sm[0]
    h = x.reshape(n, d)
    for l in range(w_in.shape[0]):
        proj = _inproj(h, norm_mix_w[l], _relayout_w_in(w_in[l]))
        ya = _hgrn2(proj, bsz, seq, lower_bounds[l], hg_norm_w[l])
        yb = _ssd(proj, bsz, seq, ssm_conv_w[l], ssm_conv_b[l], ssm_dt_bias[l], ssm_a_log[l], ssm_d[l], ssm_norm_w[l])
        yc = _rglru(proj, bsz, seq, rg_conv_w[l], rg_conv_b[l], rg_w_a[l], rg_b_a[l], rg_w_x[l], rg_b_x[l], rg_lambda[l])
        yd = _rwkv7(proj, bsz, seq, rw_mu[l], rw_w0[l], rw_w2[l], rw_a0[l], rw_a2[l], rw_g2[l], rw_k_k[l], rw_k_a[l],
                    rw_r_k[l], rw_ln_w[l], rw_ln_b[l])
        wr = jnp.pad(jnp.concatenate([router_group_w[l], router_expert_w[l]], axis=1),
                     ((0, 0), (0, LANES - N_GROUPS - N_EXPERTS)))
        br = jnp.pad(jnp.concatenate([router_group_b[l], router_expert_b[l]]), (0, LANES - N_GROUPS - N_EXPERTS)).reshape(1, LANES)
        h, xn, route, counts = _outproj_router(h, (ya, yb, yc, yd), w_out[l].astype(BF16), norm_ffn_w[l], wr, br)
        bexp, stok, sdst, nused = _dispatch_plan(route, counts, n_blocks)
        y2 = _moe_ffn(xn, _expert_weights_bf16(moe_w_gate[l]), _expert_weights_bf16(moe_w_up[l]),
                      _expert_weights_bf16(moe_w_down[l]), bexp, stok, sdst, nused)
        h = _combine(h, y2, route, final_norm_w, final=(l == w_in.shape[0] - 1))
    return h.reshape(bsz, seq, d)
```

```python
import functools
import math

import jax
import jax.numpy as jnp
from jax import lax
from jax.experimental import pallas as pl
from jax.experimental.pallas import tpu as pltpu

F32 = jnp.float32
BF16 = jnp.bfloat16

D_MODEL = 2048
GW = 512
NORM_EPS = 1e-6
HG_HEAD = 128
SSM_HEADS = 8
SSM_HEAD_DIM = 64
SSM_STATE = 128
SSM_GROUPS = 2
RG_C = 8.0
RW_HEADS = 8
RW_HEAD_DIM = 64
RW_LORA = 96
RW_GATE = 256
RW_GN_EPS = 64e-5
N_GROUPS = 4
E_PER_GROUP = 8
N_EXPERTS = 32
D_EXPERT = 1024

LANES = 128
SUBLANES = 8
VMEM_LIMIT = 56 * 1024 * 1024

C_HG_Q, C_HG_F, C_HG_I, C_HG_G = 0, 512, 1024, 1536
C_SSM_Z, C_SSM_X, C_SSM_B, C_SSM_C = 2048, 2560, 3072, 3328
C_RG_GATE, C_RG_X = 3584, 4096
C_RW_R, C_RW_K, C_RW_V = 4608, 5120, 5632
C_RW_WD, C_RW_AD, C_RW_GD = 6144, 6272, 6400
C_SSM_DT = 6656
IN_COLS_PAD = 6912

TIME_BLOCK = 256


def _cparams(sem):
    return pltpu.CompilerParams(dimension_semantics=sem, vmem_limit_bytes=VMEM_LIMIT)


def _split3(x):
    hi = x.astype(BF16)
    r1 = x - hi.astype(F32)
    mid = r1.astype(BF16)
    lo = (r1 - mid.astype(F32)).astype(BF16)
    return hi, mid, lo


def _dot_exact_rhs(x, m_bf16, dims=(((1,), (0,)), ((), ()))):
    hi, mid, lo = _split3(x)
    f = lambda a: lax.dot_general(a, m_bf16, dims, preferred_element_type=F32)
    return f(hi) + f(mid) + f(lo)


def _dot_exact_lhs(m_bf16, x, dims=(((1,), (0,)), ((), ()))):
    hi, mid, lo = _split3(x)
    f = lambda a: lax.dot_general(m_bf16, a, dims, preferred_element_type=F32)
    return f(hi) + f(mid) + f(lo)


def _dot3(a, b, dims=(((1,), (0,)), ((), ()))):
    ah = a.astype(BF16)
    al = (a - ah.astype(F32)).astype(BF16)
    bh = b.astype(BF16)
    bl = (b - bh.astype(F32)).astype(BF16)
    f = lambda p, q: lax.dot_general(p, q, dims, preferred_element_type=F32)
    return f(ah, bh) + f(ah, bl) + f(al, bh)


def _dotb(a, b, dims=(((1,), (0,)), ((), ()))):
    return lax.dot_general(a.astype(BF16), b.astype(BF16), dims, preferred_element_type=F32)


NT = (((1,), (1,)), ((), ()))
TN = (((0,), (0,)), ((), ()))


def _sigmoid(x):
    return 1.0 / (1.0 + jnp.exp(-x))


def _silu(x):
    return x * _sigmoid(x)


def _softplus(x):
    return jnp.maximum(x, 0.0) + jnp.log1p(jnp.exp(-jnp.abs(x)))


def _expm1(x):
    u = jnp.exp(x)
    return jnp.where(u == 1.0, x, jnp.where(u == 0.0, -1.0, (u - 1.0) * x / jnp.log(u)))


def _iota2(shape, axis):
    return lax.broadcasted_iota(jnp.int32, shape, axis)


def _inproj_body(x_ref, nw_ref, w_ref, o_ref, u_ref):
    @pl.when(pl.program_id(1) == 0)
    def _():
        x = x_ref[...]
        ms = jnp.mean(x * x, axis=-1, keepdims=True)
        u_ref[...] = (x * lax.rsqrt(ms + NORM_EPS) * nw_ref[...]).astype(BF16)

    o_ref[...] = jnp.dot(u_ref[...], w_ref[...], preferred_element_type=F32)


def _inproj(h, norm_w, w_pad, tm=2048, tn=768):
    n = h.shape[0]
    return pl.pallas_call(
        _inproj_body,
        grid=(n // tm, IN_COLS_PAD // tn),
        in_specs=[
            pl.BlockSpec((tm, D_MODEL), lambda i, j: (i, 0), pipeline_mode=pl.Buffered(1)),
            pl.BlockSpec((1, D_MODEL), lambda i, j: (0, 0)),
            pl.BlockSpec((D_MODEL, tn), lambda i, j: (0, j)),
        ],
        out_specs=pl.BlockSpec((tm, tn), lambda i, j: (i, j)),
        out_shape=jax.ShapeDtypeStruct((n, IN_COLS_PAD), F32),
        scratch_shapes=[pltpu.VMEM((tm, D_MODEL), BF16)],
        compiler_params=_cparams(("arbitrary", "arbitrary")),
        name="inproj",
    )(h, norm_w.reshape(1, D_MODEL), w_pad)


def _relayout_w_in(w):
    z = lambda c: jnp.zeros((w.shape[0], c), w.dtype)
    parts = [
        w[:, :3584],
        w[:, 3592:4616],
        w[:, 4616:6152],
        w[:, 6152:6248], z(32),
        w[:, 6248:6344], z(32),
        w[:, 6344:6600],
        w[:, 3584:3592], z(120),
        z(IN_COLS_PAD - 6784),
    ]
    return jnp.concatenate(parts, axis=1).astype(BF16)


def _col_spec(width, col, nl):
    blk = col // width
    assert blk * width == col
    return pl.BlockSpec((TIME_BLOCK, width), lambda b, l: (b * nl + l, blk))


def _row_spec(width):
    return pl.BlockSpec((1, width), lambda b, l: (0, 0))


def _full_spec(shape):
    return pl.BlockSpec(shape, lambda b, l: tuple(0 for _ in shape))


def _causal_conv4(buf_ref, x, w_ref, b_ref):
    tl = x.shape[0]
    buf_ref[pl.ds(SUBLANES, tl), :] = x
    y = b_ref[...] + w_ref[3:4, :] * x
    for j in range(3):
        y = y + w_ref[j:j + 1, :] * buf_ref[pl.ds(SUBLANES - 3 + j, tl), :]
    buf_ref[pl.ds(0, SUBLANES), :] = buf_ref[pl.ds(tl, SUBLANES), :]
    return y


def _rglru_body(gate_ref, x_ref, cw_ref, cb_ref, wa_ref, ba_ref, wx_ref, bx_ref, lam_ref,
                o_ref, xbuf, hcar):
    tl = TIME_BLOCK

    @pl.when(pl.program_id(1) == 0)
    def _():
        xbuf[pl.ds(0, SUBLANES), :] = jnp.zeros((SUBLANES, GW), F32)
        hcar[...] = jnp.zeros_like(hcar)

    xb = _causal_conv4(xbuf, x_ref[...], cw_ref, cb_ref)
    xb16 = xb.astype(BF16)
    r = _sigmoid(jnp.dot(xb16, wa_ref[...], preferred_element_type=F32) + ba_ref[...])
    i = _sigmoid(jnp.dot(xb16, wx_ref[...], preferred_element_type=F32) + bx_ref[...])
    log_a = -RG_C * r * _softplus(-lam_ref[...])
    a = jnp.exp(log_a)
    u = jnp.sqrt(-_expm1(2.0 * log_a)) * (i * xb)
    row = _iota2((tl, GW), 0)
    d = 1
    while d < tl:
        keep = row >= d
        a_s = jnp.where(keep, pltpu.roll(a, d, 0), 1.0)
        u_s = jnp.where(keep, pltpu.roll(u, d, 0), 0.0)
        u = a * u_s + u
        a = a * a_s
        d *= 2
    h = a * hcar[0:1, :] + u
    hcar[0:1, :] = h[tl - 1:tl, :]
    o_ref[...] = (h * jax.nn.gelu(gate_ref[...], approximate=True)).astype(BF16)


def _block_diag(w):
    nb, k, _ = w.shape
    eye = jnp.eye(nb, dtype=w.dtype)
    return (eye[:, None, :, None] * w[:, :, None, :]).reshape(nb * k, nb * k)


def _rglru(proj, bsz, seq, cw, cb, w_a, b_a, w_x, b_x, lam):
    nl = seq // TIME_BLOCK
    r = lambda v: v.reshape(1, GW)
    return pl.pallas_call(
        _rglru_body,
        grid=(bsz, nl),
        in_specs=[
            _col_spec(GW, C_RG_GATE, nl), _col_spec(GW, C_RG_X, nl),
            _full_spec((4, GW)), _row_spec(GW),
            _full_spec((GW, GW)), _row_spec(GW), _full_spec((GW, GW)), _row_spec(GW), _row_spec(GW),
        ],
        out_specs=pl.BlockSpec((TIME_BLOCK, GW), lambda b, l: (b * nl + l, 0)),
        out_shape=jax.ShapeDtypeStruct((bsz * seq, GW), BF16),
        scratch_shapes=[pltpu.VMEM((TIME_BLOCK + SUBLANES, GW), F32), pltpu.VMEM((SUBLANES, GW), F32)],
        compiler_params=_cparams(("arbitrary", "arbitrary")),
        name="rglru",
    )(proj, proj, cw, r(cb), _block_diag(w_a).astype(BF16), r(b_a), _block_diag(w_x).astype(BF16), r(b_x), r(lam))


SSD_CHUNK = 64


def _ssd_body(z_ref, x_ref, b_ref, c_ref, dt_ref, cwx_ref, cwb_ref, cwc_ref, cbx_ref, cbb_ref, cbc_ref,
              dtb_ref, alog_ref, dsk_ref, nw_ref, o_ref, xbuf, bbuf, cbuf, st_ref):
    tl, ck = TIME_BLOCK, SSD_CHUNK
    npair = SSM_HEADS // 2

    @pl.when(pl.program_id(1) == 0)
    def _():
        xbuf[pl.ds(0, SUBLANES), :] = jnp.zeros((SUBLANES, GW), F32)
        bbuf[pl.ds(0, SUBLANES), :] = jnp.zeros((SUBLANES, 2 * SSM_STATE), F32)
        cbuf[pl.ds(0, SUBLANES), :] = jnp.zeros((SUBLANES, 2 * SSM_STATE), F32)
        st_ref[...] = jnp.zeros_like(st_ref)

    xs_all = _silu(_causal_conv4(xbuf, x_ref[...], cwx_ref, cbx_ref))
    bm_all = _silu(_causal_conv4(bbuf, b_ref[...], cwb_ref, cbb_ref))
    cm_all = _silu(_causal_conv4(cbuf, c_ref[...], cwc_ref, cbc_ref))
    dt_all = _softplus(dt_ref[...] + dtb_ref[...])
    a_all = dt_all * (-jnp.exp(alog_ref[...]))

    expand = (_iota2((LANES, GW), 1) // SSM_HEAD_DIM == _iota2((LANES, GW), 0)).astype(BF16)
    tri = (_iota2((ck, ck), 0) >= _iota2((ck, ck), 1))
    tri16 = tri.astype(BF16)
    lane = _iota2((ck, LANES), 1)
    outs = []
    for c in range(tl // ck):
        sl = slice(c * ck, (c + 1) * ck)
        xs, bm, cm, dt, a = xs_all[sl], bm_all[sl], cm_all[sl], dt_all[sl], a_all[sl]
        acum = _dot_exact_lhs(tri16, a)
        acum_t = jnp.transpose(acum)
        acum_x = _dot_exact_rhs(acum, expand)
        dt_x = _dot_exact_rhs(dt, expand)
        aend_x = acum_x[ck - 1:ck, :]
        xdt = xs * dt_x
        xdec = xdt * jnp.exp(aend_x - acum_x)
        eacum = jnp.exp(acum_x)
        eend = jnp.exp(aend_x)
        ys = []
        for p in range(npair):
            g = (2 * p) // (SSM_HEADS // SSM_GROUPS)
            gs = slice(g * SSM_STATE, (g + 1) * SSM_STATE)
            ps = slice(p * LANES, (p + 1) * LANES)
            gmat = _dotb(cm[:, gs], bm[:, gs], NT)
            yd = []
            for hh in range(2):
                h = 2 * p + hh
                rel = acum[:, h:h + 1] - acum_t[h:h + 1, :]
                dec = jnp.exp(jnp.where(tri, rel, -jnp.inf))
                yd.append(_dotb(gmat * dec, xdt[:, ps]))
            y_diag = jnp.where(lane < SSM_HEAD_DIM, yd[0], yd[1])
            st = st_ref[p]
            y_off = _dotb(cm[:, gs], st) * eacum[:, ps]
            st_ref[p] = st * eend[:, ps] + _dotb(bm[:, gs], xdec[:, ps], TN)
            ys.append(y_diag + y_off)
        y = jnp.concatenate(ys, axis=1) + xs * dsk_ref[...]
        outs.append(y)
    y = jnp.concatenate(outs, axis=0) * _silu(z_ref[...])
    half = GW // SSM_GROUPS
    parts = []
    for g in range(SSM_GROUPS):
        yg = y[:, g * half:(g + 1) * half]
        parts.append(yg * lax.rsqrt(jnp.mean(yg * yg, axis=-1, keepdims=True) + NORM_EPS))
    o_ref[...] = (jnp.concatenate(parts, axis=1) * nw_ref[...]).astype(BF16)


def _ssd(proj, bsz, seq, conv_w, conv_b, dt_bias, a_log, d_skip, norm_w):
    nl = seq // TIME_BLOCK
    ns2 = 2 * SSM_STATE
    pad_row = lambda v: jnp.pad(v, (0, LANES - v.shape[0])).reshape(1, LANES)
    return pl.pallas_call(
        _ssd_body,
        grid=(bsz, nl),
        in_specs=[
            _col_spec(GW, C_SSM_Z, nl), _col_spec(GW, C_SSM_X, nl), _col_spec(ns2, C_SSM_B, nl),
            _col_spec(ns2, C_SSM_C, nl), _col_spec(LANES, C_SSM_DT, nl),
            _full_spec((4, GW)), _full_spec((4, ns2)), _full_spec((4, ns2)),
            _row_spec(GW), _row_spec(ns2), _row_spec(ns2),
            _row_spec(LANES), _row_spec(LANES), _row_spec(GW), _row_spec(GW),
        ],
        out_specs=pl.BlockSpec((TIME_BLOCK, GW), lambda b, l: (b * nl + l, 0)),
        out_shape=jax.ShapeDtypeStruct((bsz * seq, GW), BF16),
        scratch_shapes=[
            pltpu.VMEM((TIME_BLOCK + SUBLANES, GW), F32),
            pltpu.VMEM((TIME_BLOCK + SUBLANES, ns2), F32),
            pltpu.VMEM((TIME_BLOCK + SUBLANES, ns2), F32),
            pltpu.VMEM((SSM_HEADS // 2, SSM_STATE, LANES), F32),
        ],
        compiler_params=_cparams(("arbitrary", "arbitrary")),
        name="ssd",
    )(proj, proj, proj, proj, proj,
      conv_w[:, :GW], conv_w[:, GW:GW + ns2], conv_w[:, GW + ns2:],
      conv_b[:GW].reshape(1, GW), conv_b[GW:GW + ns2].reshape(1, ns2), conv_b[GW + ns2:].reshape(1, ns2),
      pad_row(dt_bias), pad_row(a_log), jnp.repeat(d_skip, SSM_HEAD_DIM).reshape(1, GW), norm_w.reshape(1, GW))


HG_CHUNK = 16


def _hgrn2_body(q_ref, f_ref, i_ref, g_ref, lb_ref, nw_ref, o_ref, st_ref):
    tl, ck = TIME_BLOCK, HG_CHUNK
    nh = GW // HG_HEAD

    @pl.when(pl.program_id(1) == 0)
    def _():
        st_ref[...] = jnp.zeros_like(st_ref)

    lb = lb_ref[...]
    row = _iota2((ck, GW), 0)
    tri16 = (_iota2((ck, ck), 0) >= _iota2((ck, ck), 1)).astype(BF16)

    def chunk(c, carry):
        r0 = pl.multiple_of(c * ck, ck)
        q = _silu(q_ref[pl.ds(r0, ck), :])
        f = lb + (1.0 - lb) * _sigmoid(f_ref[pl.ds(r0, ck), :])
        k = 1.0 - f
        v = i_ref[pl.ds(r0, ck), :]
        b = _dot_exact_lhs(tri16, jnp.log(f))
        bend = b[ck - 1:ck, :]
        qe = q * jnp.exp(b)
        kd = k * jnp.exp(bend - b)
        eend = jnp.exp(bend)
        acc = [jnp.zeros((ck, HG_HEAD), F32) for _ in range(nh)]
        for s in range(ck):
            w = q * k[s:s + 1, :] * jnp.exp(jnp.where(row >= s, b - b[s:s + 1, :], -jnp.inf))
            for h in range(nh):
                hs = slice(h * HG_HEAD, (h + 1) * HG_HEAD)
                acc[h] = acc[h] + jnp.sum(w[:, hs], axis=-1, keepdims=True) * v[s:s + 1, hs]
        outs = []
        for h in range(nh):
            hs = slice(h * HG_HEAD, (h + 1) * HG_HEAD)
            st = st_ref[h]
            o = acc[h] + _dotb(qe[:, hs], st, NT)
            st_ref[h] = st * eend[:, hs] + _dotb(v[:, hs], kd[:, hs], TN)
            outs.append(o * lax.rsqrt(jnp.mean(o * o, axis=-1, keepdims=True) + NORM_EPS))
        o = jnp.concatenate(outs, axis=1) * nw_ref[...] * _silu(g_ref[pl.ds(r0, ck), :])
        o_ref[pl.ds(r0, ck), :] = o.astype(BF16)
        return carry

    lax.fori_loop(0, tl // ck, chunk, 0, unroll=8)


def _hgrn2(proj, bsz, seq, lower_bound, norm_w):
    nl = seq // TIME_BLOCK
    return pl.pallas_call(
        _hgrn2_body,
        grid=(bsz, nl),
        in_specs=[
            _col_spec(GW, C_HG_Q, nl), _col_spec(GW, C_HG_F, nl), _col_spec(GW, C_HG_I, nl), _col_spec(GW, C_HG_G, nl),
            _row_spec(GW), _row_spec(GW),
        ],
        out_specs=pl.BlockSpec((TIME_BLOCK, GW), lambda b, l: (b * nl + l, 0)),
        out_shape=jax.ShapeDtypeStruct((bsz * seq, GW), BF16),
        scratch_shapes=[pltpu.VMEM((GW // HG_HEAD, HG_HEAD, HG_HEAD), F32)],
        compiler_params=_cparams(("arbitrary", "arbitrary")),
        name="hgrn2",
    )(proj, proj, proj, proj, lower_bound.reshape(1, GW), norm_w.reshape(1, GW))


RW_T = 32
RW_HP = 4
RW_GL = RW_HP * RW_HEAD_DIM
RW_NG = RW_HEADS // RW_HP


def _rwkv7_body(r_ref, k_ref, v_ref, wd_ref, ad_ref, gd_ref, mur_ref, muk_ref, muv_ref, muwd_ref, muad_ref, mugd_ref,
                w0_ref, w2_ref, a0_ref, a2_ref, g2_ref, kkw_ref, kaw_ref, rkw_ref, lnw_ref, lnb_ref,
                o_ref, car_r, car_k, car_v, car_wd, car_ad, car_gd, st_ref,
                kap_s, bet_s, kh_s, rho_s, v_s, c_s, y_s,
                r1_u, r2_u, mm_u, cc_u):
    tl, ct = TIME_BLOCK, RW_T
    srows = RW_HP * ct

    @pl.when(pl.program_id(1) == 0)
    def _():
        for car in (car_r, car_k, car_v, car_wd, car_ad, car_gd):
            car[...] = jnp.zeros_like(car)
        st_ref[...] = jnp.zeros_like(st_ref)

    def tshift(x_ref, car, mu_ref):
        x = x_ref[...]
        row = _iota2(x.shape, 0)
        prev = jnp.where(row == 0, car[0:1, :], pltpu.roll(x, 1, 0))
        car[0:1, :] = x[tl - 1:tl, :]
        return x + (prev - x) * mu_ref[...]

    r = tshift(r_ref, car_r, mur_ref)
    k = tshift(k_ref, car_k, muk_ref)
    v = tshift(v_ref, car_v, muv_ref)
    wd = tshift(wd_ref, car_wd, muwd_ref)
    ad = tshift(ad_ref, car_ad, muad_ref)
    gd = tshift(gd_ref, car_gd, mugd_ref)

    lw = -jnp.exp(-_softplus(-(w0_ref[...] + _dotb(jnp.tanh(wd), w2_ref[...]))) - 0.5)
    a = _sigmoid(a0_ref[...] + _dotb(ad, a2_ref[...]))
    g = _dotb(_sigmoid(gd), g2_ref[...])
    seg = (_iota2((GW, GW), 0) // RW_HEAD_DIM == _iota2((GW, GW), 1) // RW_HEAD_DIM).astype(BF16)
    kkr = k * kkw_ref[...]
    kk = kkr / jnp.maximum(jnp.sqrt(_dot_exact_rhs(kkr * kkr, seg)), 1e-12)
    k2 = k * (1.0 + (a - 1.0) * kaw_ref[...])
    blocktri = ((_iota2((tl, tl), 0) >= _iota2((tl, tl), 1))
                & (_iota2((tl, tl), 0) // ct == _iota2((tl, tl), 1) // ct)).astype(BF16)
    c = _dot_exact_lhs(blocktri, lw)
    enc = jnp.exp(-c)
    kap_s[...] = kk * jnp.exp(c - lw)
    bet_s[...] = kk * a * enc
    kh_s[...] = k2 * enc
    rho_s[...] = r * jnp.exp(c)
    v_s[...] = v
    c_s[...] = c

    rs = _iota2((srows, RW_GL), 0)
    hmask = (rs // ct) == (_iota2((srows, RW_GL), 1) // RW_HEAD_DIM)
    ri, cj = _iota2((srows, srows), 0), _iota2((srows, srows), 1)
    same = (ri // ct) == (cj // ct)
    strict = same & (ri > cj)
    incl = same & (ri >= cj)
    eye = (ri == cj).astype(F32)

    def stack(x):
        return jnp.where(hmask, jnp.concatenate([x] * RW_HP, axis=0), 0.0)

    nck = tl // ct
    bnt = (((2,), (2,)), ((0,), (0,)))
    bnn = (((2,), (1,)), ((0,), (0,)))
    btn = (((1,), (1,)), ((0,), (0,)))

    def bdot(a_, b_, dims):
        return lax.dot_general(a_.astype(BF16), b_.astype(BF16), dims, preferred_element_type=F32)

    def stack3(x):
        return jnp.where(hmask[None], jnp.concatenate([x] * RW_HP, axis=1), 0.0).astype(BF16)

    for gi in range(RW_NG):
        gl = slice(gi * RW_GL, (gi + 1) * RW_GL)
        to3 = lambda ref: ref[:, gl].reshape(nck, ct, RW_GL)
        eup = jnp.exp(to3(c_s)[:, ct - 1:ct, :])
        bet, kh = to3(bet_s), to3(kh_s)
        kaps, bets, khs = stack3(to3(kap_s)), stack3(bet), stack3(kh)
        rhos, vs = stack3(to3(rho_s)), stack3(to3(v_s))
        bet_e, kh_e = stack3(bet * eup), stack3(kh * eup)
        bk = jnp.concatenate([bets, khs], axis=1)
        kap_bk = bdot(kaps, bk, bnt)
        rho_bk = bdot(rhos, bk, bnt)
        a_ab = jnp.where(strict[None], kap_bk[:, :, :srows], 0.0)
        a_ak = jnp.where(strict[None], kap_bk[:, :, srows:], 0.0)
        a_rb = jnp.where(incl[None], rho_bk[:, :, :srows], 0.0)
        a_rk = jnp.where(incl[None], rho_bk[:, :, srows:], 0.0)
        pw = -a_ab
        inv_g = eye[None] + pw
        for _ in range(int(math.log2(ct)) - 1):
            pw = bdot(pw, pw, bnn)
            inv_g = inv_g + bdot(inv_g, pw, bnn)
        us = slice(gi * nck, (gi + 1) * nck)
        p1 = bdot(inv_g, kaps, bnn)
        p2 = bdot(inv_g, bdot(a_ak, vs, bnn), bnn)
        r1_u[us] = (rhos.astype(F32) - bdot(a_rb, p1, bnn)).astype(BF16)
        r2_u[us] = bdot(a_rk, vs, bnn) - bdot(a_rb, p2, bnn)
        mm_u[us] = bdot(p1, bet_e, btn).astype(BF16)
        cc_u[us] = bdot(vs, kh_e, btn) - bdot(p2, bet_e, btn)

    def chunk(ci, carry):
        r0 = pl.multiple_of(ci * ct, ct)
        for gi in range(RW_NG):
            un = gi * nck + ci
            gl = slice(gi * RW_GL, (gi + 1) * RW_GL)
            eup = jnp.exp(c_s[pl.ds(r0 + ct - 1, 1), gl])
            s0 = st_ref[gi]
            s16 = s0.astype(BF16)
            ys = _dotb(r1_u[un], s16, NT) + r2_u[un]
            y = ys[0:ct]
            for hh in range(1, RW_HP):
                y = y + ys[hh * ct:(hh + 1) * ct]
            y_s[pl.ds(r0, ct), gl] = y
            st_ref[gi] = s0 * eup - _dotb(s16, mm_u[un]) + cc_u[un]
        return carry

    lax.fori_loop(0, tl // ct, chunk, 0)

    y = y_s[...]
    inv_n = 1.0 / RW_HEAD_DIM
    mean = _dot_exact_rhs(y, seg) * inv_n
    yc = y - mean
    var = _dot_exact_rhs(yc * yc, seg) * inv_n
    yn = yc * lax.rsqrt(var + RW_GN_EPS) * lnw_ref[...] + lnb_ref[...]
    bonus = _dot_exact_rhs(r * k2 * rkw_ref[...], seg) * v
    o_ref[...] = ((yn + bonus) * g).astype(BF16)


def _rwkv7(proj, bsz, seq, mu, w0, w2, a0, a2, g2, k_k, k_a, r_k, ln_w, ln_b):
    nl = seq // TIME_BLOCK
    row = lambda v_: v_.reshape(1, -1)
    padl = lambda v_: jnp.pad(v_, (0, LANES - v_.shape[0])).reshape(1, LANES)
    padr = lambda m: jnp.pad(m, ((0, LANES - m.shape[0]), (0, 0))).astype(BF16)
    o = 3 * GW
    mus = [row(mu[:GW]), row(mu[GW:2 * GW]), row(mu[2 * GW:o]), padl(mu[o:o + RW_LORA]),
           padl(mu[o + RW_LORA:o + 2 * RW_LORA]), row(mu[o + 2 * RW_LORA:])]
    f32buf = lambda w: pltpu.VMEM((TIME_BLOCK, w), F32)
    car = lambda w: pltpu.VMEM((SUBLANES, w), F32)
    units = (TIME_BLOCK // RW_T) * RW_NG
    srows = RW_HP * RW_T
    return pl.pallas_call(
        _rwkv7_body,
        grid=(bsz, nl),
        in_specs=[
            _col_spec(GW, C_RW_R, nl), _col_spec(GW, C_RW_K, nl), _col_spec(GW, C_RW_V, nl),
            _col_spec(LANES, C_RW_WD, nl), _col_spec(LANES, C_RW_AD, nl), _col_spec(RW_GATE, C_RW_GD, nl),
            _row_spec(GW), _row_spec(GW), _row_spec(GW), _row_spec(LANES), _row_spec(LANES), _row_spec(RW_GATE),
            _row_spec(GW), _full_spec((LANES, GW)), _row_spec(GW), _full_spec((LANES, GW)), _full_spec((RW_GATE, GW)),
            _row_spec(GW), _row_spec(GW), _row_spec(GW), _row_spec(GW), _row_spec(GW),
        ],
        out_specs=pl.BlockSpec((TIME_BLOCK, GW), lambda b, l: (b * nl + l, 0)),
        out_shape=jax.ShapeDtypeStruct((bsz * seq, GW), BF16),
        scratch_shapes=[car(GW), car(GW), car(GW), car(LANES), car(LANES), car(RW_GATE),
                        pltpu.VMEM((RW_NG, RW_GL, RW_GL), F32)] + [f32buf(GW)] * 7 + [
                            pltpu.VMEM((units, srows, RW_GL), BF16), pltpu.VMEM((units, srows, RW_GL), F32),
                            pltpu.VMEM((units, RW_GL, RW_GL), BF16), pltpu.VMEM((units, RW_GL, RW_GL), F32)],
        compiler_params=_cparams(("arbitrary", "arbitrary")),
        name="rwkv7",
    )(proj, proj, proj, proj, proj, proj, *mus,
      row(w0), padr(w2), row(a0), padr(a2), g2.astype(BF16), row(k_k), row(k_a), row(r_k), row(ln_w), row(ln_b))


ROUTE_TM = 256
R_E1, R_E2, R_G1, R_G2, R_K1, R_K2 = 0, 1, 2, 3, 4, 5


def _outproj_router_body(h_ref, ya_ref, yb_ref, yc_ref, yd_ref, wo_ref, nw_ref, wr_ref, br_ref,
                         ho_ref, xn_ref, route_ref, cnt_ref, cnt_s):
    tm = ROUTE_TM

    @pl.when(pl.program_id(0) == 0)
    def _():
        cnt_s[...] = jnp.zeros_like(cnt_s)

    acc = h_ref[...]
    for gi, y_ref in enumerate((ya_ref, yb_ref, yc_ref, yd_ref)):
        acc = acc + jnp.dot(y_ref[...], wo_ref[gi * GW:(gi + 1) * GW, :], preferred_element_type=F32)
    ho_ref[...] = acc
    xn = acc * lax.rsqrt(jnp.mean(acc * acc, axis=-1, keepdims=True) + NORM_EPS) * nw_ref[...]
    xn_ref[...] = xn
    logits = _dot3(xn, wr_ref[...]) + br_ref[...]

    lane = _iota2((tm, LANES), 1)
    neg = -jnp.inf
    gl = jnp.where(lane < N_GROUPS, logits, neg)
    mg = jnp.max(gl, axis=-1, keepdims=True)
    g_top = 1.0 / jnp.sum(jnp.exp(gl - mg), axis=-1, keepdims=True)
    g_idx = jnp.min(jnp.where(gl == mg, lane, LANES), axis=-1, keepdims=True)
    lo = N_GROUPS + E_PER_GROUP * g_idx
    el = jnp.where(lane >= lo, jnp.where(lane < lo + E_PER_GROUP, logits, neg), neg)
    me = jnp.max(el, axis=-1, keepdims=True)
    se = jnp.sum(jnp.exp(el - me), axis=-1, keepdims=True)
    i1 = jnp.min(jnp.where(el == me, lane, LANES), axis=-1, keepdims=True)
    el2 = jnp.where(lane == i1, neg, el)
    m2 = jnp.max(el2, axis=-1, keepdims=True)
    i2 = jnp.min(jnp.where(el2 == m2, lane, LANES), axis=-1, keepdims=True)
    p1 = 1.0 / se
    p2 = jnp.exp(m2 - me) / se
    gate1 = g_top * p1 / (p1 + p2)
    gate2 = g_top * p2 / (p1 + p2)
    e1 = i1 - N_GROUPS
    e2 = i2 - N_GROUPS

    oh1 = lane == e1
    oh2 = lane == e2
    oh = jnp.where(oh1, 1.0, jnp.where(oh2, 1.0, 0.0))
    stri = (_iota2((tm, tm), 0) > _iota2((tm, tm), 1)).astype(BF16)
    before = jnp.dot(stri, oh.astype(BF16), preferred_element_type=F32) + cnt_s[0:1, :]
    k1 = jnp.sum(jnp.where(oh1, before, 0.0), axis=-1, keepdims=True)
    k2 = jnp.sum(jnp.where(oh2, before, 0.0), axis=-1, keepdims=True)
    cnt = cnt_s[0:1, :] + jnp.sum(oh, axis=0, keepdims=True)
    cnt_s[0:1, :] = cnt
    cnt_ref[...] = jnp.broadcast_to(cnt, cnt_ref.shape)

    rec = jnp.where(lane == R_E1, e1.astype(F32), 0.0)
    rec = jnp.where(lane == R_E2, e2.astype(F32), rec)
    rec = jnp.where(lane == R_G1, gate1, rec)
    rec = jnp.where(lane == R_G2, gate2, rec)
    rec = jnp.where(lane == R_K1, k1, rec)
    rec = jnp.where(lane == R_K2, k2, rec)
    route_ref[...] = rec


def _outproj_router(h, ys, w_out16, norm_w, wr, br):
    n = h.shape[0]
    tm = ROUTE_TM
    rows = lambda w: pl.BlockSpec((tm, w), lambda i: (i, 0))
    whole = lambda shape: pl.BlockSpec(shape, lambda i: (0, 0))
    return pl.pallas_call(
        _outproj_router_body,
        grid=(n // tm,),
        in_specs=[rows(D_MODEL), rows(GW), rows(GW), rows(GW), rows(GW), whole((D_MODEL, D_MODEL)),
                  whole((1, D_MODEL)), whole((D_MODEL, LANES)), whole((1, LANES))],
        out_specs=[rows(D_MODEL), rows(D_MODEL), rows(LANES), whole((SUBLANES, LANES))],
        out_shape=[jax.ShapeDtypeStruct((n, D_MODEL), F32), jax.ShapeDtypeStruct((n, D_MODEL), F32),
                   jax.ShapeDtypeStruct((n, LANES), F32), jax.ShapeDtypeStruct((SUBLANES, LANES), F32)],
        scratch_shapes=[pltpu.VMEM((SUBLANES, LANES), F32)],
        compiler_params=_cparams(("arbitrary",)),
        name="outproj_router",
    )(h, *ys, w_out16, norm_w.reshape(1, D_MODEL), wr, br)


MOE_BLK = 256


def _moe_n_blocks(n_assign):
    return (n_assign + N_EXPERTS * (MOE_BLK - 1) + MOE_BLK - 1) // MOE_BLK


def _ffn_body(bexp_ref, stok_ref, sdst_ref, nused_ref, xn_hbm, wg_ref, wu_ref, wd_ref, y_hbm,
              xbuf, obuf, gsem, ssem):
    b = pl.program_id(0)
    last_blk = pl.num_programs(0) - 2
    nused = nused_ref[0]
    blk = MOE_BLK
    dump0 = y_hbm.shape[0] - 2 * blk
    slot = b % 2
    other = 1 - slot

    def gather_row(j, r, s, prio=0):
        t = stok_ref[j * blk + r]
        pltpu.make_async_copy(xn_hbm.at[pl.ds(t, 1)], xbuf.at[s, pl.ds(r, 1)], gsem.at[s]).start(priority=prio)

    def scatter_row(j, r, s, all_dump, prio=0):
        a = sdst_ref[j * blk + r]
        dst = jnp.where(jnp.logical_or(a < 0, all_dump), dump0 + s * blk + r, a)
        pltpu.make_async_copy(obuf.at[s, pl.ds(r, 1)], y_hbm.at[pl.ds(dst, 1)], ssem.at[s]).start(priority=prio)

    def gather_wait(s):
        pltpu.make_async_copy(xn_hbm.at[pl.ds(0, blk)], xbuf.at[s], gsem.at[s]).wait()

    def scatter_wait(s):
        pltpu.make_async_copy(obuf.at[s], y_hbm.at[pl.ds(0, blk)], ssem.at[s]).wait()

    @pl.when(b == 0)
    def _():
        obuf[...] = jnp.zeros_like(obuf)
        for s in range(2):
            pltpu.make_async_copy(obuf.at[s], y_hbm.at[pl.ds(dump0 + s * blk, blk)], ssem.at[s]).start()
        for s in range(2):
            pltpu.make_async_copy(obuf.at[s], y_hbm.at[pl.ds(dump0 + s * blk, blk)], ssem.at[s]).wait()

        def row(r, c):
            gather_row(0, r, 0)
            return c
        lax.fori_loop(0, blk, row, 0, unroll=8)

    @pl.when(b < nused)
    def _():
        gather_wait(slot)

        @pl.when(b >= 1)
        def _():
            scatter_wait(slot)

        nxt = jnp.minimum(b + 1, last_blk)
        prv = jnp.maximum(b - 1, 0)
        for r in range(blk):
            gather_row(nxt, r, other, r % 2)
        for r in range(blk):
            scatter_row(prv, r, other, b == 0, r % 2)
        x = xbuf[slot].astype(BF16)
        hid = _silu(jnp.dot(x, wg_ref[0], preferred_element_type=F32)) * jnp.dot(x, wu_ref[0], preferred_element_type=F32)
        obuf[slot] = jnp.dot(hid.astype(BF16), wd_ref[0], preferred_element_type=F32)

    @pl.when(b == nused)
    def _():
        gather_wait(slot)
        scatter_wait(slot)

        def row(r, c):
            scatter_row(b - 1, r, other, False)
            return c
        lax.fori_loop(0, blk, row, 0, unroll=8)
        scatter_wait(other)


def _moe_ffn(xn, wg16, wu16, wd16, bexp, stok, sdst, nused):
    n = xn.shape[0]
    n_steps = bexp.shape[0]
    wspec = lambda shape: pl.BlockSpec((1,) + shape, lambda b, be, st, sd, nu: (be[b], 0, 0))
    return pl.pallas_call(
        _ffn_body,
        grid_spec=pltpu.PrefetchScalarGridSpec(
            num_scalar_prefetch=4,
            grid=(n_steps,),
            in_specs=[pl.BlockSpec(memory_space=pl.ANY), wspec((D_MODEL, D_EXPERT)), wspec((D_MODEL, D_EXPERT)),
                      wspec((D_EXPERT, D_MODEL))],
            out_specs=pl.BlockSpec(memory_space=pl.ANY),
            scratch_shapes=[pltpu.VMEM((2, MOE_BLK, D_MODEL), F32), pltpu.VMEM((2, MOE_BLK, D_MODEL), F32),
                            pltpu.SemaphoreType.DMA((2,)), pltpu.SemaphoreType.DMA((2,))],
        ),
        out_shape=jax.ShapeDtypeStruct((2 * n + 2 * MOE_BLK, D_MODEL), F32),
        compiler_params=_cparams(("arbitrary",)),
        name="moe_ffn",
    )(bexp, stok, sdst, nused, xn, wg16, wu16, wd16)


def _cast_body(x_ref, o_ref):
    o_ref[...] = x_ref[...].astype(BF16)


def _expert_weights_bf16(w, layer):
    _, e, a, b_ = w.shape
    return pl.pallas_call(
        _cast_body, grid=(e,),
        in_specs=[pl.BlockSpec((None, 1, a, b_), lambda i: (layer, i, 0, 0))],
        out_specs=pl.BlockSpec((1, a, b_), lambda i: (i, 0, 0)),
        out_shape=jax.ShapeDtypeStruct((e, a, b_), BF16),
        compiler_params=_cparams(("arbitrary",)), name="cast_bf16",
    )(w)


def _dispatch_plan(route, counts, n_blocks):
    n = route.shape[0]
    eid = route[:, R_E1:R_E2 + 1].astype(jnp.int32)
    rank = route[:, R_K1:R_K2 + 1].astype(jnp.int32)
    cnt = counts[0, :N_EXPERTS].astype(jnp.int32)
    padded = (cnt + MOE_BLK - 1) // MOE_BLK * MOE_BLK
    pad_end = jnp.cumsum(padded)
    pad_start = pad_end - padded
    onehot = eid[:, :, None] == jnp.arange(N_EXPERTS, dtype=jnp.int32)
    slot = (jnp.sum(jnp.where(onehot, pad_start, 0), axis=-1) + rank).reshape(-1)
    n_slots = n_blocks * MOE_BLK
    asg = jnp.full((n_slots,), -1, jnp.int32).at[slot].set(jnp.arange(2 * n, dtype=jnp.int32))
    stok = jnp.maximum(asg, 0) // 2
    sdst = jnp.where(asg >= 0, (asg % 2) * n + stok, -1)
    starts = jnp.arange(n_blocks + 1, dtype=jnp.int32) * MOE_BLK
    bexp = jnp.minimum(jnp.sum(pad_end[None, :] <= starts[:, None], axis=1), N_EXPERTS - 1)
    nused = (pad_end[-1] // MOE_BLK).reshape(1).astype(jnp.int32)
    return bexp.astype(jnp.int32), stok, sdst, nused


def _combine_body(h_ref, y1_ref, y2_ref, route_ref, nw_ref, o_ref, *, final):
    rt = route_ref[...]
    out = h_ref[...] + rt[:, R_G1:R_G1 + 1] * y1_ref[...] + rt[:, R_G2:R_G2 + 1] * y2_ref[...]
    if final:
        out = out * lax.rsqrt(jnp.mean(out * out, axis=-1, keepdims=True) + NORM_EPS) * nw_ref[...]
    o_ref[...] = out


def _combine(h, y2, route, norm_w, final, tm=256):
    n = h.shape[0]
    return pl.pallas_call(
        functools.partial(_combine_body, final=final),
        grid=(n // tm,),
        in_specs=[pl.BlockSpec((tm, D_MODEL), lambda i: (i, 0)), pl.BlockSpec((tm, D_MODEL), lambda i: (i, 0)),
                  pl.BlockSpec((tm, D_MODEL), lambda i: (i + n // tm, 0)),
                  pl.BlockSpec((tm, LANES), lambda i: (i, 0)), pl.BlockSpec((1, D_MODEL), lambda i: (0, 0))],
        out_specs=pl.BlockSpec((tm, D_MODEL), lambda i: (i, 0)),
        out_shape=jax.ShapeDtypeStruct((n, D_MODEL), F32),
        compiler_params=_cparams(("arbitrary",)),
        name="combine",
    )(h, y2, y2, route, norm_w.reshape(1, D_MODEL))


def kernel(x, w_in, w_out, norm_mix_w, norm_ffn_w, final_norm_w, hg_lb_param, hg_norm_w, ssm_conv_w, ssm_conv_b, ssm_dt_bias, ssm_a_log, ssm_d, ssm_norm_w, rg_conv_w, rg_conv_b, rg_w_a, rg_b_a, rg_w_x, rg_b_x, rg_lambda, rw_mu, rw_w0, rw_w2, rw_a0, rw_a2, rw_g2, rw_k_k, rw_k_a, rw_r_k, rw_ln_w, rw_ln_b, router_group_w, router_group_b, router_expert_w, router_expert_b, moe_w_gate, moe_w_up, moe_w_down):
    bsz, seq, d = x.shape
    n = bsz * seq
    n_blocks = _moe_n_blocks(2 * n)
    sm = jax.nn.softmax(hg_lb_param.astype(F32), axis=0)
    lower_bounds = jnp.cumsum(sm, axis=0) - sm[0]
    h = x.reshape(n, d)
    for l in range(w_in.shape[0]):
        proj = _inproj(h, norm_mix_w[l], _relayout_w_in(w_in[l]))
        ya = _hgrn2(proj, bsz, seq, lower_bounds[l], hg_norm_w[l])
        yb = _ssd(proj, bsz, seq, ssm_conv_w[l], ssm_conv_b[l], ssm_dt_bias[l], ssm_a_log[l], ssm_d[l], ssm_norm_w[l])
        yc = _rglru(proj, bsz, seq, rg_conv_w[l], rg_conv_b[l], rg_w_a[l], rg_b_a[l], rg_w_x[l], rg_b_x[l], rg_lambda[l])
        yd = _rwkv7(proj, bsz, seq, rw_mu[l], rw_w0[l], rw_w2[l], rw_a0[l], rw_a2[l], rw_g2[l], rw_k_k[l], rw_k_a[l],
                    rw_r_k[l], rw_ln_w[l], rw_ln_b[l])
        wr = jnp.pad(jnp.concatenate([router_group_w[l], router_expert_w[l]], axis=1),
                     ((0, 0), (0, LANES - N_GROUPS - N_EXPERTS)))
        br = jnp.pad(jnp.concatenate([router_group_b[l], router_expert_b[l]]), (0, LANES - N_GROUPS - N_EXPERTS)).reshape(1, LANES)
        h, xn, route, counts = _outproj_router(h, (ya, yb, yc, yd), w_out[l].astype(BF16), norm_ffn_w[l], wr, br)
        bexp, stok, sdst, nused = _dispatch_plan(route, counts, n_blocks)
        y2 = _moe_ffn(xn, _expert_weights_bf16(moe_w_gate, l), _expert_weights_bf16(moe_w_up, l),
                      _expert_weights_bf16(moe_w_down, l), bexp, stok, sdst, nused)
        h = _combine(h, y2, route, final_norm_w, final=(l == w_in.shape[0] - 1))
    return h.reshape(bsz, seq, d)
```

```python
import functools
import math

import jax
import jax.numpy as jnp
from jax import lax
from jax.experimental import pallas as pl
from jax.experimental.pallas import tpu as pltpu

F32 = jnp.float32
BF16 = jnp.bfloat16

D_MODEL = 2048
GW = 512
NORM_EPS = 1e-6
HG_HEAD = 128
SSM_HEADS = 8
SSM_HEAD_DIM = 64
SSM_STATE = 128
SSM_GROUPS = 2
RG_C = 8.0
RW_HEADS = 8
RW_HEAD_DIM = 64
RW_LORA = 96
RW_GATE = 256
RW_GN_EPS = 64e-5
N_GROUPS = 4
E_PER_GROUP = 8
N_EXPERTS = 32
D_EXPERT = 1024

LANES = 128
SUBLANES = 8
VMEM_LIMIT = 56 * 1024 * 1024

C_HG_Q, C_HG_F, C_HG_I, C_HG_G = 0, 512, 1024, 1536
C_SSM_Z, C_SSM_X, C_SSM_B, C_SSM_C = 2048, 2560, 3072, 3328
C_RG_GATE, C_RG_X = 3584, 4096
C_RW_R, C_RW_K, C_RW_V = 4608, 5120, 5632
C_RW_WD, C_RW_AD, C_RW_GD = 6144, 6272, 6400
C_SSM_DT = 6656
IN_COLS_PAD = 6912

TIME_BLOCK = 256


def _cparams(sem):
    return pltpu.CompilerParams(dimension_semantics=sem, vmem_limit_bytes=VMEM_LIMIT)


def _split3(x):
    hi = x.astype(BF16)
    r1 = x - hi.astype(F32)
    mid = r1.astype(BF16)
    lo = (r1 - mid.astype(F32)).astype(BF16)
    return hi, mid, lo


def _dot_exact_rhs(x, m_bf16, dims=(((1,), (0,)), ((), ()))):
    hi, mid, lo = _split3(x)
    f = lambda a: lax.dot_general(a, m_bf16, dims, preferred_element_type=F32)
    return f(hi) + f(mid) + f(lo)


def _dot_exact_lhs(m_bf16, x, dims=(((1,), (0,)), ((), ()))):
    hi, mid, lo = _split3(x)
    f = lambda a: lax.dot_general(m_bf16, a, dims, preferred_element_type=F32)
    return f(hi) + f(mid) + f(lo)


def _dot3(a, b, dims=(((1,), (0,)), ((), ()))):
    ah = a.astype(BF16)
    al = (a - ah.astype(F32)).astype(BF16)
    bh = b.astype(BF16)
    bl = (b - bh.astype(F32)).astype(BF16)
    f = lambda p, q: lax.dot_general(p, q, dims, preferred_element_type=F32)
    return f(ah, bh) + f(ah, bl) + f(al, bh)


def _dotb(a, b, dims=(((1,), (0,)), ((), ()))):
    return lax.dot_general(a.astype(BF16), b.astype(BF16), dims, preferred_element_type=F32)


NT = (((1,), (1,)), ((), ()))
TN = (((0,), (0,)), ((), ()))


def _sigmoid(x):
    return 1.0 / (1.0 + jnp.exp(-x))


def _silu(x):
    return x * _sigmoid(x)


def _softplus(x):
    return jnp.maximum(x, 0.0) + jnp.log1p(jnp.exp(-jnp.abs(x)))


def _expm1(x):
    u = jnp.exp(x)
    return jnp.where(u == 1.0, x, jnp.where(u == 0.0, -1.0, (u - 1.0) * x / jnp.log(u)))


def _iota2(shape, axis):
    return lax.broadcasted_iota(jnp.int32, shape, axis)


def _inproj_body(x_ref, nw_ref, w_ref, o_ref, u_ref):
    @pl.when(pl.program_id(1) == 0)
    def _():
        x = x_ref[...]
        ms = jnp.mean(x * x, axis=-1, keepdims=True)
        u_ref[...] = (x * lax.rsqrt(ms + NORM_EPS) * nw_ref[...]).astype(BF16)

    o_ref[...] = jnp.dot(u_ref[...], w_ref[...], preferred_element_type=F32)


def _inproj(h, norm_w, w_pad, tm=2048, tn=768):
    n = h.shape[0]
    return pl.pallas_call(
        _inproj_body,
        grid=(n // tm, IN_COLS_PAD // tn),
        in_specs=[
            pl.BlockSpec((tm, D_MODEL), lambda i, j: (i, 0), pipeline_mode=pl.Buffered(1)),
            pl.BlockSpec((1, D_MODEL), lambda i, j: (0, 0)),
            pl.BlockSpec((D_MODEL, tn), lambda i, j: (0, j)),
        ],
        out_specs=pl.BlockSpec((tm, tn), lambda i, j: (i, j)),
        out_shape=jax.ShapeDtypeStruct((n, IN_COLS_PAD), F32),
        scratch_shapes=[pltpu.VMEM((tm, D_MODEL), BF16)],
        compiler_params=_cparams(("arbitrary", "arbitrary")),
        name="inproj",
    )(h, norm_w.reshape(1, D_MODEL), w_pad)


def _relayout_w_in(w):
    z = lambda c: jnp.zeros((w.shape[0], c), w.dtype)
    parts = [
        w[:, :3584],
        w[:, 3592:4616],
        w[:, 4616:6152],
        w[:, 6152:6248], z(32),
        w[:, 6248:6344], z(32),
        w[:, 6344:6600],
        w[:, 3584:3592], z(120),
        z(IN_COLS_PAD - 6784),
    ]
    return jnp.concatenate(parts, axis=1).astype(BF16)


def _col_spec(width, col, nl):
    blk = col // width
    assert blk * width == col
    return pl.BlockSpec((TIME_BLOCK, width), lambda b, l: (b * nl + l, blk))


def _row_spec(width):
    return pl.BlockSpec((1, width), lambda b, l: (0, 0))


def _full_spec(shape):
    return pl.BlockSpec(shape, lambda b, l: tuple(0 for _ in shape))


def _causal_conv4(buf_ref, x, w_ref, b_ref):
    tl = x.shape[0]
    buf_ref[pl.ds(SUBLANES, tl), :] = x
    y = b_ref[...] + w_ref[3:4, :] * x
    for j in range(3):
        y = y + w_ref[j:j + 1, :] * buf_ref[pl.ds(SUBLANES - 3 + j, tl), :]
    buf_ref[pl.ds(0, SUBLANES), :] = buf_ref[pl.ds(tl, SUBLANES), :]
    return y


def _rglru_body(gate_ref, x_ref, cw_ref, cb_ref, wa_ref, ba_ref, wx_ref, bx_ref, lam_ref,
                o_ref, xbuf, hcar):
    tl = TIME_BLOCK

    @pl.when(pl.program_id(1) == 0)
    def _():
        xbuf[pl.ds(0, SUBLANES), :] = jnp.zeros((SUBLANES, GW), F32)
        hcar[...] = jnp.zeros_like(hcar)

    xb = _causal_conv4(xbuf, x_ref[...], cw_ref, cb_ref)
    xb16 = xb.astype(BF16)
    r = _sigmoid(jnp.dot(xb16, wa_ref[...], preferred_element_type=F32) + ba_ref[...])
    i = _sigmoid(jnp.dot(xb16, wx_ref[...], preferred_element_type=F32) + bx_ref[...])
    log_a = -RG_C * r * _softplus(-lam_ref[...])
    a = jnp.exp(log_a)
    u = jnp.sqrt(-_expm1(2.0 * log_a)) * (i * xb)
    row = _iota2((tl, GW), 0)
    d = 1
    while d < tl:
        keep = row >= d
        a_s = jnp.where(keep, pltpu.roll(a, d, 0), 1.0)
        u_s = jnp.where(keep, pltpu.roll(u, d, 0), 0.0)
        u = a * u_s + u
        a = a * a_s
        d *= 2
    h = a * hcar[0:1, :] + u
    hcar[0:1, :] = h[tl - 1:tl, :]
    o_ref[...] = (h * jax.nn.gelu(gate_ref[...], approximate=True)).astype(BF16)


def _block_diag(w):
    nb, k, _ = w.shape
    eye = jnp.eye(nb, dtype=w.dtype)
    return (eye[:, None, :, None] * w[:, :, None, :]).reshape(nb * k, nb * k)


def _rglru(proj, bsz, seq, cw, cb, w_a, b_a, w_x, b_x, lam):
    nl = seq // TIME_BLOCK
    r = lambda v: v.reshape(1, GW)
    return pl.pallas_call(
        _rglru_body,
        grid=(bsz, nl),
        in_specs=[
            _col_spec(GW, C_RG_GATE, nl), _col_spec(GW, C_RG_X, nl),
            _full_spec((4, GW)), _row_spec(GW),
            _full_spec((GW, GW)), _row_spec(GW), _full_spec((GW, GW)), _row_spec(GW), _row_spec(GW),
        ],
        out_specs=pl.BlockSpec((TIME_BLOCK, GW), lambda b, l: (b * nl + l, 0)),
        out_shape=jax.ShapeDtypeStruct((bsz * seq, GW), BF16),
        scratch_shapes=[pltpu.VMEM((TIME_BLOCK + SUBLANES, GW), F32), pltpu.VMEM((SUBLANES, GW), F32)],
        compiler_params=_cparams(("arbitrary", "arbitrary")),
        name="rglru",
    )(proj, proj, cw, r(cb), _block_diag(w_a).astype(BF16), r(b_a), _block_diag(w_x).astype(BF16), r(b_x), r(lam))


SSD_CHUNK = 64


def _ssd_body(z_ref, x_ref, b_ref, c_ref, dt_ref, cwx_ref, cwb_ref, cwc_ref, cbx_ref, cbb_ref, cbc_ref,
              dtb_ref, alog_ref, dsk_ref, nw_ref, o_ref, xbuf, bbuf, cbuf, st_ref):
    tl, ck = TIME_BLOCK, SSD_CHUNK
    npair = SSM_HEADS // 2

    @pl.when(pl.program_id(1) == 0)
    def _():
        xbuf[pl.ds(0, SUBLANES), :] = jnp.zeros((SUBLANES, GW), F32)
        bbuf[pl.ds(0, SUBLANES), :] = jnp.zeros((SUBLANES, 2 * SSM_STATE), F32)
        cbuf[pl.ds(0, SUBLANES), :] = jnp.zeros((SUBLANES, 2 * SSM_STATE), F32)
        st_ref[...] = jnp.zeros_like(st_ref)

    xs_all = _silu(_causal_conv4(xbuf, x_ref[...], cwx_ref, cbx_ref))
    bm_all = _silu(_causal_conv4(bbuf, b_ref[...], cwb_ref, cbb_ref))
    cm_all = _silu(_causal_conv4(cbuf, c_ref[...], cwc_ref, cbc_ref))
    dt_all = _softplus(dt_ref[...] + dtb_ref[...])
    a_all = dt_all * (-jnp.exp(alog_ref[...]))

    expand = (_iota2((LANES, GW), 1) // SSM_HEAD_DIM == _iota2((LANES, GW), 0)).astype(BF16)
    tri = (_iota2((ck, ck), 0) >= _iota2((ck, ck), 1))
    tri16 = tri.astype(BF16)
    lane = _iota2((ck, LANES), 1)
    outs = []
    for c in range(tl // ck):
        sl = slice(c * ck, (c + 1) * ck)
        xs, bm, cm, dt, a = xs_all[sl], bm_all[sl], cm_all[sl], dt_all[sl], a_all[sl]
        acum = _dot_exact_lhs(tri16, a)
        acum_t = jnp.transpose(acum)
        acum_x = _dot_exact_rhs(acum, expand)
        dt_x = _dot_exact_rhs(dt, expand)
        aend_x = acum_x[ck - 1:ck, :]
        xdt = xs * dt_x
        xdec = xdt * jnp.exp(aend_x - acum_x)
        eacum = jnp.exp(acum_x)
        eend = jnp.exp(aend_x)
        ys = []
        for p in range(npair):
            g = (2 * p) // (SSM_HEADS // SSM_GROUPS)
            gs = slice(g * SSM_STATE, (g + 1) * SSM_STATE)
            ps = slice(p * LANES, (p + 1) * LANES)
            gmat = _dotb(cm[:, gs], bm[:, gs], NT)
            yd = []
            for hh in range(2):
                h = 2 * p + hh
                rel = acum[:, h:h + 1] - acum_t[h:h + 1, :]
                dec = jnp.exp(jnp.where(tri, rel, -jnp.inf))
                yd.append(_dotb(gmat * dec, xdt[:, ps]))
            y_diag = jnp.where(lane < SSM_HEAD_DIM, yd[0], yd[1])
            st = st_ref[p]
            y_off = _dotb(cm[:, gs], st) * eacum[:, ps]
            st_ref[p] = st * eend[:, ps] + _dotb(bm[:, gs], xdec[:, ps], TN)
            ys.append(y_diag + y_off)
        y = jnp.concatenate(ys, axis=1) + xs * dsk_ref[...]
        outs.append(y)
    y = jnp.concatenate(outs, axis=0) * _silu(z_ref[...])
    half = GW // SSM_GROUPS
    parts = []
    for g in range(SSM_GROUPS):
        yg = y[:, g * half:(g + 1) * half]
        parts.append(yg * lax.rsqrt(jnp.mean(yg * yg, axis=-1, keepdims=True) + NORM_EPS))
    o_ref[...] = (jnp.concatenate(parts, axis=1) * nw_ref[...]).astype(BF16)


def _ssd(proj, bsz, seq, conv_w, conv_b, dt_bias, a_log, d_skip, norm_w):
    nl = seq // TIME_BLOCK
    ns2 = 2 * SSM_STATE
    pad_row = lambda v: jnp.pad(v, (0, LANES - v.shape[0])).reshape(1, LANES)
    return pl.pallas_call(
        _ssd_body,
        grid=(bsz, nl),
        in_specs=[
            _col_spec(GW, C_SSM_Z, nl), _col_spec(GW, C_SSM_X, nl), _col_spec(ns2, C_SSM_B, nl),
            _col_spec(ns2, C_SSM_C, nl), _col_spec(LANES, C_SSM_DT, nl),
            _full_spec((4, GW)), _full_spec((4, ns2)), _full_spec((4, ns2)),
            _row_spec(GW), _row_spec(ns2), _row_spec(ns2),
            _row_spec(LANES), _row_spec(LANES), _row_spec(GW), _row_spec(GW),
        ],
        out_specs=pl.BlockSpec((TIME_BLOCK, GW), lambda b, l: (b * nl + l, 0)),
        out_shape=jax.ShapeDtypeStruct((bsz * seq, GW), BF16),
        scratch_shapes=[
            pltpu.VMEM((TIME_BLOCK + SUBLANES, GW), F32),
            pltpu.VMEM((TIME_BLOCK + SUBLANES, ns2), F32),
            pltpu.VMEM((TIME_BLOCK + SUBLANES, ns2), F32),
            pltpu.VMEM((SSM_HEADS // 2, SSM_STATE, LANES), F32),
        ],
        compiler_params=_cparams(("arbitrary", "arbitrary")),
        name="ssd",
    )(proj, proj, proj, proj, proj,
      conv_w[:, :GW], conv_w[:, GW:GW + ns2], conv_w[:, GW + ns2:],
      conv_b[:GW].reshape(1, GW), conv_b[GW:GW + ns2].reshape(1, ns2), conv_b[GW + ns2:].reshape(1, ns2),
      pad_row(dt_bias), pad_row(a_log), jnp.repeat(d_skip, SSM_HEAD_DIM).reshape(1, GW), norm_w.reshape(1, GW))


HG_CHUNK = 16


def _hgrn2_body(q_ref, f_ref, i_ref, g_ref, lb_ref, nw_ref, o_ref, st_ref):
    tl, ck = TIME_BLOCK, HG_CHUNK
    nh = GW // HG_HEAD

    @pl.when(pl.program_id(1) == 0)
    def _():
        st_ref[...] = jnp.zeros_like(st_ref)

    lb = lb_ref[...]
    row8 = _iota2((ck // 2, GW), 0)
    tri16 = (_iota2((ck, ck), 0) >= _iota2((ck, ck), 1)).astype(BF16)

    def chunk(c, carry):
        r0 = pl.multiple_of(c * ck, ck)
        q = _silu(q_ref[pl.ds(r0, ck), :])
        f = lb + (1.0 - lb) * _sigmoid(f_ref[pl.ds(r0, ck), :])
        k = 1.0 - f
        v = i_ref[pl.ds(r0, ck), :]
        b = _dot_exact_lhs(tri16, jnp.log(f))
        bend = b[ck - 1:ck, :]
        qe = q * jnp.exp(b)
        kd = k * jnp.exp(bend - b)
        eend = jnp.exp(bend)
        half = ck // 2
        qh, bh = (q[:half], q[half:]), (b[:half], b[half:])
        acc = [[jnp.zeros((half, HG_HEAD), F32) for _ in range(nh)] for _ in range(2)]
        for s in range(ck):
            for p in range(s // half, 2):
                rel = bh[p] - b[s:s + 1, :]
                if p == s // half:
                    rel = jnp.where(row8 >= s - p * half, rel, -jnp.inf)
                w = qh[p] * k[s:s + 1, :] * jnp.exp(rel)
                for h in range(nh):
                    hs = slice(h * HG_HEAD, (h + 1) * HG_HEAD)
                    acc[p][h] = acc[p][h] + jnp.sum(w[:, hs], axis=-1, keepdims=True) * v[s:s + 1, hs]
        outs = []
        for h in range(nh):
            hs = slice(h * HG_HEAD, (h + 1) * HG_HEAD)
            st = st_ref[h]
            o = jnp.concatenate([acc[0][h], acc[1][h]], axis=0) + _dotb(qe[:, hs], st, NT)
            st_ref[h] = st * eend[:, hs] + _dotb(v[:, hs], kd[:, hs], TN)
            outs.append(o * lax.rsqrt(jnp.mean(o * o, axis=-1, keepdims=True) + NORM_EPS))
        o = jnp.concatenate(outs, axis=1) * nw_ref[...] * _silu(g_ref[pl.ds(r0, ck), :])
        o_ref[pl.ds(r0, ck), :] = o.astype(BF16)
        return carry

    lax.fori_loop(0, tl // ck, chunk, 0, unroll=8)


def _hgrn2(proj, bsz, seq, lower_bound, norm_w):
    nl = seq // TIME_BLOCK
    return pl.pallas_call(
        _hgrn2_body,
        grid=(bsz, nl),
        in_specs=[
            _col_spec(GW, C_HG_Q, nl), _col_spec(GW, C_HG_F, nl), _col_spec(GW, C_HG_I, nl), _col_spec(GW, C_HG_G, nl),
            _row_spec(GW), _row_spec(GW),
        ],
        out_specs=pl.BlockSpec((TIME_BLOCK, GW), lambda b, l: (b * nl + l, 0)),
        out_shape=jax.ShapeDtypeStruct((bsz * seq, GW), BF16),
        scratch_shapes=[pltpu.VMEM((GW // HG_HEAD, HG_HEAD, HG_HEAD), F32)],
        compiler_params=_cparams(("arbitrary", "arbitrary")),
        name="hgrn2",
    )(proj, proj, proj, proj, lower_bound.reshape(1, GW), norm_w.reshape(1, GW))


RW_T = 64
RW_HP = 2
RW_GL = RW_HP * RW_HEAD_DIM
RW_NG = RW_HEADS // RW_HP
RW_GB = 4


def _rwkv7_body(r_ref, k_ref, v_ref, wd_ref, ad_ref, gd_ref, mur_ref, muk_ref, muv_ref, muwd_ref, muad_ref, mugd_ref,
                w0_ref, w2_ref, a0_ref, a2_ref, g2_ref, kkw_ref, kaw_ref, rkw_ref, lnw_ref, lnb_ref,
                o_ref, car_r, car_k, car_v, car_wd, car_ad, car_gd, st_ref,
                kap_s, bet_s, kh_s, rho_s, v_s, c_s, y_s,
                r1_u, r2_u, mm_u, cc_u):
    tl, ct = TIME_BLOCK, RW_T
    srows = RW_HP * ct

    @pl.when(pl.program_id(1) == 0)
    def _():
        for car in (car_r, car_k, car_v, car_wd, car_ad, car_gd):
            car[...] = jnp.zeros_like(car)
        st_ref[...] = jnp.zeros_like(st_ref)

    def tshift(x_ref, car, mu_ref):
        x = x_ref[...]
        row = _iota2(x.shape, 0)
        prev = jnp.where(row == 0, car[0:1, :], pltpu.roll(x, 1, 0))
        car[0:1, :] = x[tl - 1:tl, :]
        return x + (prev - x) * mu_ref[...]

    r = tshift(r_ref, car_r, mur_ref)
    k = tshift(k_ref, car_k, muk_ref)
    v = tshift(v_ref, car_v, muv_ref)
    wd = tshift(wd_ref, car_wd, muwd_ref)
    ad = tshift(ad_ref, car_ad, muad_ref)
    gd = tshift(gd_ref, car_gd, mugd_ref)

    lw = -jnp.exp(-_softplus(-(w0_ref[...] + _dotb(jnp.tanh(wd), w2_ref[...]))) - 0.5)
    a = _sigmoid(a0_ref[...] + _dotb(ad, a2_ref[...]))
    g = _dotb(_sigmoid(gd), g2_ref[...])
    seg = (_iota2((GW, GW), 0) // RW_HEAD_DIM == _iota2((GW, GW), 1) // RW_HEAD_DIM).astype(BF16)
    kkr = k * kkw_ref[...]
    kk = kkr / jnp.maximum(jnp.sqrt(_dot_exact_rhs(kkr * kkr, seg)), 1e-12)
    k2 = k * (1.0 + (a - 1.0) * kaw_ref[...])
    blocktri = ((_iota2((tl, tl), 0) >= _iota2((tl, tl), 1))
                & (_iota2((tl, tl), 0) // ct == _iota2((tl, tl), 1) // ct)).astype(BF16)
    c = _dot_exact_lhs(blocktri, lw)
    enc = jnp.exp(-c)
    kap_s[...] = kk * jnp.exp(c - lw)
    bet_s[...] = kk * a * enc
    kh_s[...] = k2 * enc
    rho_s[...] = r * jnp.exp(c)
    v_s[...] = v
    c_s[...] = c

    rs = _iota2((srows, RW_GL), 0)
    hmask = (rs // ct) == (_iota2((srows, RW_GL), 1) // RW_HEAD_DIM)
    ri, cj = _iota2((srows, srows), 0), _iota2((srows, srows), 1)
    same = (ri // ct) == (cj // ct)
    strict = same & (ri > cj)
    incl = same & (ri >= cj)
    eye = (ri == cj).astype(F32)

    def stack(x):
        return jnp.where(hmask, jnp.concatenate([x] * RW_HP, axis=0), 0.0)

    nck = tl // ct
    bnt = (((2,), (2,)), ((0,), (0,)))
    bnn = (((2,), (1,)), ((0,), (0,)))
    btn = (((1,), (1,)), ((0,), (0,)))

    def bdot(a_, b_, dims):
        return lax.dot_general(a_.astype(BF16), b_.astype(BF16), dims, preferred_element_type=F32)

    def stack3(x):
        return jnp.where(hmask[None], jnp.concatenate([x] * RW_HP, axis=1), 0.0).astype(BF16)

    for g0 in range(0, RW_NG, RW_GB):
        to3 = lambda ref: jnp.concatenate(
            [ref[:, gi * RW_GL:(gi + 1) * RW_GL].reshape(nck, ct, RW_GL) for gi in range(g0, g0 + RW_GB)], axis=0)
        eup = jnp.exp(to3(c_s)[:, ct - 1:ct, :])
        bet, kh = to3(bet_s), to3(kh_s)
        kaps, bets, khs = stack3(to3(kap_s)), stack3(bet), stack3(kh)
        rhos, vs = stack3(to3(rho_s)), stack3(to3(v_s))
        bet_e, kh_e = stack3(bet * eup), stack3(kh * eup)
        bk = jnp.concatenate([bets, khs], axis=1)
        kap_bk = bdot(kaps, bk, bnt)
        rho_bk = bdot(rhos, bk, bnt)
        a_ab = jnp.where(strict[None], kap_bk[:, :, :srows], 0.0)
        a_ak = jnp.where(strict[None], kap_bk[:, :, srows:], 0.0)
        a_rb = jnp.where(incl[None], rho_bk[:, :, :srows], 0.0)
        a_rk = jnp.where(incl[None], rho_bk[:, :, srows:], 0.0)
        pw = -a_ab
        inv_g = eye[None] + pw
        for _ in range(int(math.log2(ct)) - 1):
            pw = bdot(pw, pw, bnn)
            inv_g = inv_g + bdot(inv_g, pw, bnn)
        us = slice(g0 * nck, (g0 + RW_GB) * nck)
        p1 = bdot(inv_g, kaps, bnn)
        p2 = bdot(inv_g, bdot(a_ak, vs, bnn), bnn)
        r1_u[us] = (rhos.astype(F32) - bdot(a_rb, p1, bnn)).astype(BF16)
        r2_u[us] = bdot(a_rk, vs, bnn) - bdot(a_rb, p2, bnn)
        mm_u[us] = bdot(p1, bet_e, btn).astype(BF16)
        cc_u[us] = bdot(vs, kh_e, btn) - bdot(p2, bet_e, btn)

    def chunk(ci, carry):
        r0 = pl.multiple_of(ci * ct, ct)
        for gi in range(RW_NG):
            un = gi * nck + ci
            gl = slice(gi * RW_GL, (gi + 1) * RW_GL)
            tail = c_s[pl.ds(pl.multiple_of(r0 + ct - SUBLANES, SUBLANES), SUBLANES), gl]
            eup = jnp.exp(tail[SUBLANES - 1:SUBLANES, :])
            s0 = st_ref[gi]
            s16 = s0.astype(BF16)
            ys = _dotb(r1_u[un], s16, NT) + r2_u[un]
            y = ys[0:ct]
            for hh in range(1, RW_HP):
                y = y + ys[hh * ct:(hh + 1) * ct]
            y_s[pl.ds(r0, ct), gl] = y
            st_ref[gi] = s0 * eup - _dotb(s16, mm_u[un]) + cc_u[un]
        return carry

    lax.fori_loop(0, tl // ct, chunk, 0)

    y = y_s[...]
    inv_n = 1.0 / RW_HEAD_DIM
    mean = _dot_exact_rhs(y, seg) * inv_n
    yc = y - mean
    var = _dot_exact_rhs(yc * yc, seg) * inv_n
    yn = yc * lax.rsqrt(var + RW_GN_EPS) * lnw_ref[...] + lnb_ref[...]
    bonus = _dot_exact_rhs(r * k2 * rkw_ref[...], seg) * v
    o_ref[...] = ((yn + bonus) * g).astype(BF16)


def _rwkv7(proj, bsz, seq, mu, w0, w2, a0, a2, g2, k_k, k_a, r_k, ln_w, ln_b):
    nl = seq // TIME_BLOCK
    row = lambda v_: v_.reshape(1, -1)
    padl = lambda v_: jnp.pad(v_, (0, LANES - v_.shape[0])).reshape(1, LANES)
    padr = lambda m: jnp.pad(m, ((0, LANES - m.shape[0]), (0, 0))).astype(BF16)
    o = 3 * GW
    mus = [row(mu[:GW]), row(mu[GW:2 * GW]), row(mu[2 * GW:o]), padl(mu[o:o + RW_LORA]),
           padl(mu[o + RW_LORA:o + 2 * RW_LORA]), row(mu[o + 2 * RW_LORA:])]
    f32buf = lambda w: pltpu.VMEM((TIME_BLOCK, w), F32)
    car = lambda w: pltpu.VMEM((SUBLANES, w), F32)
    units = (TIME_BLOCK // RW_T) * RW_NG
    srows = RW_HP * RW_T
    return pl.pallas_call(
        _rwkv7_body,
        grid=(bsz, nl),
        in_specs=[
            _col_spec(GW, C_RW_R, nl), _col_spec(GW, C_RW_K, nl), _col_spec(GW, C_RW_V, nl),
            _col_spec(LANES, C_RW_WD, nl), _col_spec(LANES, C_RW_AD, nl), _col_spec(RW_GATE, C_RW_GD, nl),
            _row_spec(GW), _row_spec(GW), _row_spec(GW), _row_spec(LANES), _row_spec(LANES), _row_spec(RW_GATE),
            _row_spec(GW), _full_spec((LANES, GW)), _row_spec(GW), _full_spec((LANES, GW)), _full_spec((RW_GATE, GW)),
            _row_spec(GW), _row_spec(GW), _row_spec(GW), _row_spec(GW), _row_spec(GW),
        ],
        out_specs=pl.BlockSpec((TIME_BLOCK, GW), lambda b, l: (b * nl + l, 0)),
        out_shape=jax.ShapeDtypeStruct((bsz * seq, GW), BF16),
        scratch_shapes=[car(GW), car(GW), car(GW), car(LANES), car(LANES), car(RW_GATE),
                        pltpu.VMEM((RW_NG, RW_GL, RW_GL), F32)] + [f32buf(GW)] * 7 + [
                            pltpu.VMEM((units, srows, RW_GL), BF16), pltpu.VMEM((units, srows, RW_GL), F32),
                            pltpu.VMEM((units, RW_GL, RW_GL), BF16), pltpu.VMEM((units, RW_GL, RW_GL), F32)],
        compiler_params=_cparams(("arbitrary", "arbitrary")),
        name="rwkv7",
    )(proj, proj, proj, proj, proj, proj, *mus,
      row(w0), padr(w2), row(a0), padr(a2), g2.astype(BF16), row(k_k), row(k_a), row(r_k), row(ln_w), row(ln_b))


ROUTE_TM = 256
R_E1, R_E2, R_G1, R_G2, R_K1, R_K2 = 0, 1, 2, 3, 4, 5


def _outproj_router_body(h_ref, ya_ref, yb_ref, yc_ref, yd_ref, wo_ref, nw_ref, wr_ref, br_ref,
                         ho_ref, xn_ref, route_ref, cnt_ref, cnt_s):
    tm = ROUTE_TM

    @pl.when(pl.program_id(0) == 0)
    def _():
        cnt_s[...] = jnp.zeros_like(cnt_s)

    acc = h_ref[...]
    for gi, y_ref in enumerate((ya_ref, yb_ref, yc_ref, yd_ref)):
        acc = acc + jnp.dot(y_ref[...], wo_ref[gi * GW:(gi + 1) * GW, :], preferred_element_type=F32)
    ho_ref[...] = acc
    xn = acc * lax.rsqrt(jnp.mean(acc * acc, axis=-1, keepdims=True) + NORM_EPS) * nw_ref[...]
    xn_ref[...] = xn
    logits = _dot3(xn, wr_ref[...]) + br_ref[...]

    lane = _iota2((tm, LANES), 1)
    neg = -jnp.inf
    gl = jnp.where(lane < N_GROUPS, logits, neg)
    mg = jnp.max(gl, axis=-1, keepdims=True)
    g_top = 1.0 / jnp.sum(jnp.exp(gl - mg), axis=-1, keepdims=True)
    g_idx = jnp.min(jnp.where(gl == mg, lane, LANES), axis=-1, keepdims=True)
    lo = N_GROUPS + E_PER_GROUP * g_idx
    el = jnp.where(lane >= lo, jnp.where(lane < lo + E_PER_GROUP, logits, neg), neg)
    me = jnp.max(el, axis=-1, keepdims=True)
    se = jnp.sum(jnp.exp(el - me), axis=-1, keepdims=True)
    i1 = jnp.min(jnp.where(el == me, lane, LANES), axis=-1, keepdims=True)
    el2 = jnp.where(lane == i1, neg, el)
    m2 = jnp.max(el2, axis=-1, keepdims=True)
    i2 = jnp.min(jnp.where(el2 == m2, lane, LANES), axis=-1, keepdims=True)
    p1 = 1.0 / se
    p2 = jnp.exp(m2 - me) / se
    gate1 = g_top * p1 / (p1 + p2)
    gate2 = g_top * p2 / (p1 + p2)
    e1 = i1 - N_GROUPS
    e2 = i2 - N_GROUPS

    oh1 = lane == e1
    oh2 = lane == e2
    oh = jnp.where(oh1, 1.0, jnp.where(oh2, 1.0, 0.0))
    stri = (_iota2((tm, tm), 0) > _iota2((tm, tm), 1)).astype(BF16)
    before = jnp.dot(stri, oh.astype(BF16), preferred_element_type=F32) + cnt_s[0:1, :]
    k1 = jnp.sum(jnp.where(oh1, before, 0.0), axis=-1, keepdims=True)
    k2 = jnp.sum(jnp.where(oh2, before, 0.0), axis=-1, keepdims=True)
    cnt = cnt_s[0:1, :] + jnp.sum(oh, axis=0, keepdims=True)
    cnt_s[0:1, :] = cnt
    cnt_ref[...] = jnp.broadcast_to(cnt, cnt_ref.shape)

    rec = jnp.where(lane == R_E1, e1.astype(F32), 0.0)
    rec = jnp.where(lane == R_E2, e2.astype(F32), rec)
    rec = jnp.where(lane == R_G1, gate1, rec)
    rec = jnp.where(lane == R_G2, gate2, rec)
    rec = jnp.where(lane == R_K1, k1, rec)
    rec = jnp.where(lane == R_K2, k2, rec)
    route_ref[...] = rec


def _outproj_router(h, ys, w_out16, norm_w, wr, br):
    n = h.shape[0]
    tm = ROUTE_TM
    rows = lambda w: pl.BlockSpec((tm, w), lambda i: (i, 0))
    whole = lambda shape: pl.BlockSpec(shape, lambda i: (0, 0))
    return pl.pallas_call(
        _outproj_router_body,
        grid=(n // tm,),
        in_specs=[rows(D_MODEL), rows(GW), rows(GW), rows(GW), rows(GW), whole((D_MODEL, D_MODEL)),
                  whole((1, D_MODEL)), whole((D_MODEL, LANES)), whole((1, LANES))],
        out_specs=[rows(D_MODEL), rows(D_MODEL), rows(LANES), whole((SUBLANES, LANES))],
        out_shape=[jax.ShapeDtypeStruct((n, D_MODEL), F32), jax.ShapeDtypeStruct((n, D_MODEL), F32),
                   jax.ShapeDtypeStruct((n, LANES), F32), jax.ShapeDtypeStruct((SUBLANES, LANES), F32)],
        scratch_shapes=[pltpu.VMEM((SUBLANES, LANES), F32)],
        compiler_params=_cparams(("arbitrary",)),
        name="outproj_router",
    )(h, *ys, w_out16, norm_w.reshape(1, D_MODEL), wr, br)


MOE_BLK = 256


def _moe_n_blocks(n_assign):
    return (n_assign + N_EXPERTS * (MOE_BLK - 1) + MOE_BLK - 1) // MOE_BLK


def _ffn_body(bexp_ref, stok_ref, sdst_ref, nused_ref, xn_hbm, wg_ref, wu_ref, wd_ref, y_hbm,
              xbuf, obuf, gsem, ssem):
    b = pl.program_id(0)
    last_blk = pl.num_programs(0) - 2
    nused = nused_ref[0]
    blk = MOE_BLK
    dump0 = y_hbm.shape[0] - 2 * blk
    slot = b % 2
    other = 1 - slot

    def gather_row(j, r, s, prio=0):
        t = stok_ref[j * blk + r]
        pltpu.make_async_copy(xn_hbm.at[pl.ds(t, 1)], xbuf.at[s, pl.ds(r, 1)], gsem.at[s]).start(priority=prio)

    def scatter_row(j, r, s, all_dump, prio=0):
        a = sdst_ref[j * blk + r]
        dst = jnp.where(jnp.logical_or(a < 0, all_dump), dump0 + s * blk + r, a)
        pltpu.make_async_copy(obuf.at[s, pl.ds(r, 1)], y_hbm.at[pl.ds(dst, 1)], ssem.at[s]).start(priority=prio)

    def gather_wait(s):
        pltpu.make_async_copy(xn_hbm.at[pl.ds(0, blk)], xbuf.at[s], gsem.at[s]).wait()

    def scatter_wait(s):
        pltpu.make_async_copy(obuf.at[s], y_hbm.at[pl.ds(0, blk)], ssem.at[s]).wait()

    @pl.when(b == 0)
    def _():
        obuf[...] = jnp.zeros_like(obuf)
        for s in range(2):
            pltpu.make_async_copy(obuf.at[s], y_hbm.at[pl.ds(dump0 + s * blk, blk)], ssem.at[s]).start()
        for s in range(2):
            pltpu.make_async_copy(obuf.at[s], y_hbm.at[pl.ds(dump0 + s * blk, blk)], ssem.at[s]).wait()

        def row(r, c):
            gather_row(0, r, 0)
            return c
        lax.fori_loop(0, blk, row, 0, unroll=8)

    @pl.when(b < nused)
    def _():
        gather_wait(slot)

        @pl.when(b >= 1)
        def _():
            scatter_wait(slot)

        nxt = jnp.minimum(b + 1, last_blk)
        prv = jnp.maximum(b - 1, 0)
        for r in range(blk):
            gather_row(nxt, r, other, r % 2)
        for r in range(blk):
            scatter_row(prv, r, other, b == 0, r % 2)
        x = xbuf[slot].astype(BF16)
        hid = _silu(jnp.dot(x, wg_ref[0], preferred_element_type=F32)) * jnp.dot(x, wu_ref[0], preferred_element_type=F32)
        obuf[slot] = jnp.dot(hid.astype(BF16), wd_ref[0], preferred_element_type=F32)

    @pl.when(b == nused)
    def _():
        gather_wait(slot)
        scatter_wait(slot)

        def row(r, c):
            scatter_row(b - 1, r, other, False)
            return c
        lax.fori_loop(0, blk, row, 0, unroll=8)
        scatter_wait(other)


def _moe_ffn(xn, wg16, wu16, wd16, bexp, stok, sdst, nused):
    n = xn.shape[0]
    n_steps = bexp.shape[0]
    wspec = lambda shape: pl.BlockSpec((1,) + shape, lambda b, be, st, sd, nu: (be[b], 0, 0))
    return pl.pallas_call(
        _ffn_body,
        grid_spec=pltpu.PrefetchScalarGridSpec(
            num_scalar_prefetch=4,
            grid=(n_steps,),
            in_specs=[pl.BlockSpec(memory_space=pl.ANY), wspec((D_MODEL, D_EXPERT)), wspec((D_MODEL, D_EXPERT)),
                      wspec((D_EXPERT, D_MODEL))],
            out_specs=pl.BlockSpec(memory_space=pl.ANY),
            scratch_shapes=[pltpu.VMEM((2, MOE_BLK, D_MODEL), F32), pltpu.VMEM((2, MOE_BLK, D_MODEL), F32),
                            pltpu.SemaphoreType.DMA((2,)), pltpu.SemaphoreType.DMA((2,))],
        ),
        out_shape=jax.ShapeDtypeStruct((2 * n + 2 * MOE_BLK, D_MODEL), F32),
        compiler_params=_cparams(("arbitrary",)),
        name="moe_ffn",
    )(bexp, stok, sdst, nused, xn, wg16, wu16, wd16)


def _cast_body(x_ref, o_ref):
    o_ref[...] = x_ref[...].astype(BF16)


def _expert_weights_bf16(w, layer):
    _, e, a, b_ = w.shape
    return pl.pallas_call(
        _cast_body, grid=(e,),
        in_specs=[pl.BlockSpec((None, 1, a, b_), lambda i: (layer, i, 0, 0))],
        out_specs=pl.BlockSpec((1, a, b_), lambda i: (i, 0, 0)),
        out_shape=jax.ShapeDtypeStruct((e, a, b_), BF16),
        compiler_params=_cparams(("arbitrary",)), name="cast_bf16",
    )(w)


def _dispatch_plan(route, counts, n_blocks):
    n = route.shape[0]
    eid = route[:, R_E1:R_E2 + 1].astype(jnp.int32)
    rank = route[:, R_K1:R_K2 + 1].astype(jnp.int32)
    cnt = counts[0, :N_EXPERTS].astype(jnp.int32)
    padded = (cnt + MOE_BLK - 1) // MOE_BLK * MOE_BLK
    pad_end = jnp.cumsum(padded)
    pad_start = pad_end - padded
    onehot = eid[:, :, None] == jnp.arange(N_EXPERTS, dtype=jnp.int32)
    slot = (jnp.sum(jnp.where(onehot, pad_start, 0), axis=-1) + rank).reshape(-1)
    n_slots = n_blocks * MOE_BLK
    asg = jnp.full((n_slots,), -1, jnp.int32).at[slot].set(jnp.arange(2 * n, dtype=jnp.int32))
    stok = jnp.maximum(asg, 0) // 2
    sdst = jnp.where(asg >= 0, (asg % 2) * n + stok, -1)
    starts = jnp.arange(n_blocks + 1, dtype=jnp.int32) * MOE_BLK
    bexp = jnp.minimum(jnp.sum(pad_end[None, :] <= starts[:, None], axis=1), N_EXPERTS - 1)
    nused = (pad_end[-1] // MOE_BLK).reshape(1).astype(jnp.int32)
    return bexp.astype(jnp.int32), stok, sdst, nused


def _combine_body(h_ref, y1_ref, y2_ref, route_ref, nw_ref, o_ref, *, final):
    rt = route_ref[...]
    out = h_ref[...] + rt[:, R_G1:R_G1 + 1] * y1_ref[...] + rt[:, R_G2:R_G2 + 1] * y2_ref[...]
    if final:
        out = out * lax.rsqrt(jnp.mean(out * out, axis=-1, keepdims=True) + NORM_EPS) * nw_ref[...]
    o_ref[...] = out


def _combine(h, y2, route, norm_w, final, tm=256):
    n = h.shape[0]
    return pl.pallas_call(
        functools.partial(_combine_body, final=final),
        grid=(n // tm,),
        in_specs=[pl.BlockSpec((tm, D_MODEL), lambda i: (i, 0)), pl.BlockSpec((tm, D_MODEL), lambda i: (i, 0)),
                  pl.BlockSpec((tm, D_MODEL), lambda i: (i + n // tm, 0)),
                  pl.BlockSpec((tm, LANES), lambda i: (i, 0)), pl.BlockSpec((1, D_MODEL), lambda i: (0, 0))],
        out_specs=pl.BlockSpec((tm, D_MODEL), lambda i: (i, 0)),
        out_shape=jax.ShapeDtypeStruct((n, D_MODEL), F32),
        compiler_params=_cparams(("arbitrary",)),
        name="combine",
    )(h, y2, y2, route, norm_w.reshape(1, D_MODEL))


def kernel(x, w_in, w_out, norm_mix_w, norm_ffn_w, final_norm_w, hg_lb_param, hg_norm_w, ssm_conv_w, ssm_conv_b, ssm_dt_bias, ssm_a_log, ssm_d, ssm_norm_w, rg_conv_w, rg_conv_b, rg_w_a, rg_b_a, rg_w_x, rg_b_x, rg_lambda, rw_mu, rw_w0, rw_w2, rw_a0, rw_a2, rw_g2, rw_k_k, rw_k_a, rw_r_k, rw_ln_w, rw_ln_b, router_group_w, router_group_b, router_expert_w, router_expert_b, moe_w_gate, moe_w_up, moe_w_down):
    bsz, seq, d = x.shape
    n = bsz * seq
    n_blocks = _moe_n_blocks(2 * n)
    sm = jax.nn.softmax(hg_lb_param.astype(F32), axis=0)
    lower_bounds = jnp.cumsum(sm, axis=0) - sm[0]
    h = x.reshape(n, d)
    for l in range(w_in.shape[0]):
        proj = _inproj(h, norm_mix_w[l], _relayout_w_in(w_in[l]))
        ya = _hgrn2(proj, bsz, seq, lower_bounds[l], hg_norm_w[l])
        yb = _ssd(proj, bsz, seq, ssm_conv_w[l], ssm_conv_b[l], ssm_dt_bias[l], ssm_a_log[l], ssm_d[l], ssm_norm_w[l])
        yc = _rglru(proj, bsz, seq, rg_conv_w[l], rg_conv_b[l], rg_w_a[l], rg_b_a[l], rg_w_x[l], rg_b_x[l], rg_lambda[l])
        yd = _rwkv7(proj, bsz, seq, rw_mu[l], rw_w0[l], rw_w2[l], rw_a0[l], rw_a2[l], rw_g2[l], rw_k_k[l], rw_k_a[l],
                    rw_r_k[l], rw_ln_w[l], rw_ln_b[l])
        wr = jnp.pad(jnp.concatenate([router_group_w[l], router_expert_w[l]], axis=1),
                     ((0, 0), (0, LANES - N_GROUPS - N_EXPERTS)))
        br = jnp.pad(jnp.concatenate([router_group_b[l], router_expert_b[l]]), (0, LANES - N_GROUPS - N_EXPERTS)).reshape(1, LANES)
        h, xn, route, counts = _outproj_router(h, (ya, yb, yc, yd), w_out[l].astype(BF16), norm_ffn_w[l], wr, br)
        bexp, stok, sdst, nused = _dispatch_plan(route, counts, n_blocks)
        y2 = _moe_ffn(xn, _expert_weights_bf16(moe_w_gate, l), _expert_weights_bf16(moe_w_up, l),
                      _expert_weights_bf16(moe_w_down, l), bexp, stok, sdst, nused)
        h = _combine(h, y2, route, final_norm_w, final=(l == w_in.shape[0] - 1))
    return h.reshape(bsz, seq, d)
```

```python
import functools
import math

import jax
import jax.numpy as jnp
from jax import lax
from jax.experimental import pallas as pl
from jax.experimental.pallas import tpu as pltpu

F32 = jnp.float32
BF16 = jnp.bfloat16

D_MODEL = 2048
GW = 512
NORM_EPS = 1e-6
HG_HEAD = 128
SSM_HEADS = 8
SSM_HEAD_DIM = 64
SSM_STATE = 128
SSM_GROUPS = 2
RG_C = 8.0
RW_HEADS = 8
RW_HEAD_DIM = 64
RW_LORA = 96
RW_GATE = 256
RW_GN_EPS = 64e-5
N_GROUPS = 4
E_PER_GROUP = 8
N_EXPERTS = 32
D_EXPERT = 1024

LANES = 128
SUBLANES = 8
VMEM_LIMIT = 56 * 1024 * 1024

C_HG_Q, C_HG_F, C_HG_I, C_HG_G = 0, 512, 1024, 1536
C_SSM_Z, C_SSM_X, C_SSM_B, C_SSM_C = 2048, 2560, 3072, 3328
C_RG_GATE, C_RG_X = 3584, 4096
C_RW_R, C_RW_K, C_RW_V = 4608, 5120, 5632
C_RW_WD, C_RW_AD, C_RW_GD = 6144, 6272, 6400
C_SSM_DT = 6656
IN_COLS_PAD = 6912

TIME_BLOCK = 256


def _cparams(sem):
    return pltpu.CompilerParams(dimension_semantics=sem, vmem_limit_bytes=VMEM_LIMIT)


def _split3(x):
    hi = x.astype(BF16)
    r1 = x - hi.astype(F32)
    mid = r1.astype(BF16)
    lo = (r1 - mid.astype(F32)).astype(BF16)
    return hi, mid, lo


def _dot_exact_rhs(x, m_bf16, dims=(((1,), (0,)), ((), ()))):
    hi, mid, lo = _split3(x)
    f = lambda a: lax.dot_general(a, m_bf16, dims, preferred_element_type=F32)
    return f(hi) + f(mid) + f(lo)


def _dot_exact_lhs(m_bf16, x, dims=(((1,), (0,)), ((), ()))):
    hi, mid, lo = _split3(x)
    f = lambda a: lax.dot_general(m_bf16, a, dims, preferred_element_type=F32)
    return f(hi) + f(mid) + f(lo)


def _dot3(a, b, dims=(((1,), (0,)), ((), ()))):
    ah = a.astype(BF16)
    al = (a - ah.astype(F32)).astype(BF16)
    bh = b.astype(BF16)
    bl = (b - bh.astype(F32)).astype(BF16)
    f = lambda p, q: lax.dot_general(p, q, dims, preferred_element_type=F32)
    return f(ah, bh) + f(ah, bl) + f(al, bh)


def _dotb(a, b, dims=(((1,), (0,)), ((), ()))):
    return lax.dot_general(a.astype(BF16), b.astype(BF16), dims, preferred_element_type=F32)


NT = (((1,), (1,)), ((), ()))
TN = (((0,), (0,)), ((), ()))


def _sigmoid(x):
    return 1.0 / (1.0 + jnp.exp(-x))


def _silu(x):
    return x * _sigmoid(x)


def _softplus(x):
    return jnp.maximum(x, 0.0) + jnp.log1p(jnp.exp(-jnp.abs(x)))


def _expm1(x):
    u = jnp.exp(x)
    return jnp.where(u == 1.0, x, jnp.where(u == 0.0, -1.0, (u - 1.0) * x / jnp.log(u)))


def _iota2(shape, axis):
    return lax.broadcasted_iota(jnp.int32, shape, axis)


def _inproj_body(x_ref, nw_ref, w_ref, o_ref, u_ref):
    @pl.when(pl.program_id(1) == 0)
    def _():
        x = x_ref[...]
        ms = jnp.mean(x * x, axis=-1, keepdims=True)
        u_ref[...] = (x * lax.rsqrt(ms + NORM_EPS) * nw_ref[...]).astype(BF16)

    o_ref[...] = jnp.dot(u_ref[...], w_ref[...], preferred_element_type=F32)


def _inproj(h, norm_w, w_pad, tm=2048, tn=768):
    n = h.shape[0]
    return pl.pallas_call(
        _inproj_body,
        grid=(n // tm, IN_COLS_PAD // tn),
        in_specs=[
            pl.BlockSpec((tm, D_MODEL), lambda i, j: (i, 0), pipeline_mode=pl.Buffered(1)),
            pl.BlockSpec((1, D_MODEL), lambda i, j: (0, 0)),
            pl.BlockSpec((D_MODEL, tn), lambda i, j: (0, j)),
        ],
        out_specs=pl.BlockSpec((tm, tn), lambda i, j: (i, j)),
        out_shape=jax.ShapeDtypeStruct((n, IN_COLS_PAD), F32),
        scratch_shapes=[pltpu.VMEM((tm, D_MODEL), BF16)],
        compiler_params=_cparams(("arbitrary", "arbitrary")),
        name="inproj",
    )(h, norm_w.reshape(1, D_MODEL), w_pad)


def _relayout_w_in(w):
    z = lambda c: jnp.zeros((w.shape[0], c), w.dtype)
    parts = [
        w[:, :3584],
        w[:, 3592:4616],
        w[:, 4616:6152],
        w[:, 6152:6248], z(32),
        w[:, 6248:6344], z(32),
        w[:, 6344:6600],
        w[:, 3584:3592], z(120),
        z(IN_COLS_PAD - 6784),
    ]
    return jnp.concatenate(parts, axis=1).astype(BF16)


def _col_spec(width, col, nl):
    blk = col // width
    assert blk * width == col
    return pl.BlockSpec((TIME_BLOCK, width), lambda b, l: (b * nl + l, blk))


def _row_spec(width):
    return pl.BlockSpec((1, width), lambda b, l: (0, 0))


def _full_spec(shape):
    return pl.BlockSpec(shape, lambda b, l: tuple(0 for _ in shape))


def _causal_conv4(buf_ref, x, w_ref, b_ref):
    tl = x.shape[0]
    buf_ref[pl.ds(SUBLANES, tl), :] = x
    y = b_ref[...] + w_ref[3:4, :] * x
    for j in range(3):
        y = y + w_ref[j:j + 1, :] * buf_ref[pl.ds(SUBLANES - 3 + j, tl), :]
    buf_ref[pl.ds(0, SUBLANES), :] = buf_ref[pl.ds(tl, SUBLANES), :]
    return y


def _rglru_body(gate_ref, x_ref, cw_ref, cb_ref, wa_ref, ba_ref, wx_ref, bx_ref, lam_ref,
                o_ref, xbuf, hcar):
    tl = TIME_BLOCK

    @pl.when(pl.program_id(1) == 0)
    def _():
        xbuf[pl.ds(0, SUBLANES), :] = jnp.zeros((SUBLANES, GW), F32)
        hcar[...] = jnp.zeros_like(hcar)

    xb = _causal_conv4(xbuf, x_ref[...], cw_ref, cb_ref)
    xb16 = xb.astype(BF16)
    r = _sigmoid(jnp.dot(xb16, wa_ref[...], preferred_element_type=F32) + ba_ref[...])
    i = _sigmoid(jnp.dot(xb16, wx_ref[...], preferred_element_type=F32) + bx_ref[...])
    log_a = -RG_C * r * _softplus(-lam_ref[...])
    a = jnp.exp(log_a)
    u = jnp.sqrt(-_expm1(2.0 * log_a)) * (i * xb)
    row = _iota2((tl, GW), 0)
    d = 1
    while d < tl:
        keep = row >= d
        a_s = jnp.where(keep, pltpu.roll(a, d, 0), 1.0)
        u_s = jnp.where(keep, pltpu.roll(u, d, 0), 0.0)
        u = a * u_s + u
        a = a * a_s
        d *= 2
    h = a * hcar[0:1, :] + u
    hcar[0:1, :] = h[tl - 1:tl, :]
    o_ref[...] = (h * jax.nn.gelu(gate_ref[...], approximate=True)).astype(BF16)


def _block_diag(w):
    nb, k, _ = w.shape
    eye = jnp.eye(nb, dtype=w.dtype)
    return (eye[:, None, :, None] * w[:, :, None, :]).reshape(nb * k, nb * k)


def _rglru(proj, bsz, seq, cw, cb, w_a, b_a, w_x, b_x, lam):
    nl = seq // TIME_BLOCK
    r = lambda v: v.reshape(1, GW)
    return pl.pallas_call(
        _rglru_body,
        grid=(bsz, nl),
        in_specs=[
            _col_spec(GW, C_RG_GATE, nl), _col_spec(GW, C_RG_X, nl),
            _full_spec((4, GW)), _row_spec(GW),
            _full_spec((GW, GW)), _row_spec(GW), _full_spec((GW, GW)), _row_spec(GW), _row_spec(GW),
        ],
        out_specs=pl.BlockSpec((TIME_BLOCK, GW), lambda b, l: (b * nl + l, 0)),
        out_shape=jax.ShapeDtypeStruct((bsz * seq, GW), BF16),
        scratch_shapes=[pltpu.VMEM((TIME_BLOCK + SUBLANES, GW), F32), pltpu.VMEM((SUBLANES, GW), F32)],
        compiler_params=_cparams(("arbitrary", "arbitrary")),
        name="rglru",
    )(proj, proj, cw, r(cb), _block_diag(w_a).astype(BF16), r(b_a), _block_diag(w_x).astype(BF16), r(b_x), r(lam))


SSD_CHUNK = 64


def _ssd_body(z_ref, x_ref, b_ref, c_ref, dt_ref, cwx_ref, cwb_ref, cwc_ref, cbx_ref, cbb_ref, cbc_ref,
              dtb_ref, alog_ref, dsk_ref, nw_ref, o_ref, xbuf, bbuf, cbuf, st_ref):
    tl, ck = TIME_BLOCK, SSD_CHUNK
    npair = SSM_HEADS // 2

    @pl.when(pl.program_id(1) == 0)
    def _():
        xbuf[pl.ds(0, SUBLANES), :] = jnp.zeros((SUBLANES, GW), F32)
        bbuf[pl.ds(0, SUBLANES), :] = jnp.zeros((SUBLANES, 2 * SSM_STATE), F32)
        cbuf[pl.ds(0, SUBLANES), :] = jnp.zeros((SUBLANES, 2 * SSM_STATE), F32)
        st_ref[...] = jnp.zeros_like(st_ref)

    xs_all = _silu(_causal_conv4(xbuf, x_ref[...], cwx_ref, cbx_ref))
    bm_all = _silu(_causal_conv4(bbuf, b_ref[...], cwb_ref, cbb_ref))
    cm_all = _silu(_causal_conv4(cbuf, c_ref[...], cwc_ref, cbc_ref))
    dt_all = _softplus(dt_ref[...] + dtb_ref[...])
    a_all = dt_all * (-jnp.exp(alog_ref[...]))

    expand = (_iota2((LANES, GW), 1) // SSM_HEAD_DIM == _iota2((LANES, GW), 0)).astype(BF16)
    tri = (_iota2((ck, ck), 0) >= _iota2((ck, ck), 1))
    tri16 = tri.astype(BF16)
    lane = _iota2((ck, LANES), 1)
    outs = []
    for c in range(tl // ck):
        sl = slice(c * ck, (c + 1) * ck)
        xs, bm, cm, dt, a = xs_all[sl], bm_all[sl], cm_all[sl], dt_all[sl], a_all[sl]
        acum = _dot_exact_lhs(tri16, a)
        acum_t = jnp.transpose(acum)
        acum_x = _dot_exact_rhs(acum, expand)
        dt_x = _dot_exact_rhs(dt, expand)
        aend_x = acum_x[ck - 1:ck, :]
        xdt = xs * dt_x
        xdec = xdt * jnp.exp(aend_x - acum_x)
        eacum = jnp.exp(acum_x)
        eend = jnp.exp(aend_x)
        ys = []
        for p in range(npair):
            g = (2 * p) // (SSM_HEADS // SSM_GROUPS)
            gs = slice(g * SSM_STATE, (g + 1) * SSM_STATE)
            ps = slice(p * LANES, (p + 1) * LANES)
            gmat = _dotb(cm[:, gs], bm[:, gs], NT)
            yd = []
            for hh in range(2):
                h = 2 * p + hh
                rel = acum[:, h:h + 1] - acum_t[h:h + 1, :]
                dec = jnp.exp(jnp.where(tri, rel, -jnp.inf))
                yd.append(_dotb(gmat * dec, xdt[:, ps]))
            y_diag = jnp.where(lane < SSM_HEAD_DIM, yd[0], yd[1])
            st = st_ref[p]
            y_off = _dotb(cm[:, gs], st) * eacum[:, ps]
            st_ref[p] = st * eend[:, ps] + _dotb(bm[:, gs], xdec[:, ps], TN)
            ys.append(y_diag + y_off)
        y = jnp.concatenate(ys, axis=1) + xs * dsk_ref[...]
        outs.append(y)
    y = jnp.concatenate(outs, axis=0) * _silu(z_ref[...])
    half = GW // SSM_GROUPS
    parts = []
    for g in range(SSM_GROUPS):
        yg = y[:, g * half:(g + 1) * half]
        parts.append(yg * lax.rsqrt(jnp.mean(yg * yg, axis=-1, keepdims=True) + NORM_EPS))
    o_ref[...] = (jnp.concatenate(parts, axis=1) * nw_ref[...]).astype(BF16)


def _ssd(proj, bsz, seq, conv_w, conv_b, dt_bias, a_log, d_skip, norm_w):
    nl = seq // TIME_BLOCK
    ns2 = 2 * SSM_STATE
    pad_row = lambda v: jnp.pad(v, (0, LANES - v.shape[0])).reshape(1, LANES)
    return pl.pallas_call(
        _ssd_body,
        grid=(bsz, nl),
        in_specs=[
            _col_spec(GW, C_SSM_Z, nl), _col_spec(GW, C_SSM_X, nl), _col_spec(ns2, C_SSM_B, nl),
            _col_spec(ns2, C_SSM_C, nl), _col_spec(LANES, C_SSM_DT, nl),
            _full_spec((4, GW)), _full_spec((4, ns2)), _full_spec((4, ns2)),
            _row_spec(GW), _row_spec(ns2), _row_spec(ns2),
            _row_spec(LANES), _row_spec(LANES), _row_spec(GW), _row_spec(GW),
        ],
        out_specs=pl.BlockSpec((TIME_BLOCK, GW), lambda b, l: (b * nl + l, 0)),
        out_shape=jax.ShapeDtypeStruct((bsz * seq, GW), BF16),
        scratch_shapes=[
            pltpu.VMEM((TIME_BLOCK + SUBLANES, GW), F32),
            pltpu.VMEM((TIME_BLOCK + SUBLANES, ns2), F32),
            pltpu.VMEM((TIME_BLOCK + SUBLANES, ns2), F32),
            pltpu.VMEM((SSM_HEADS // 2, SSM_STATE, LANES), F32),
        ],
        compiler_params=_cparams(("arbitrary", "arbitrary")),
        name="ssd",
    )(proj, proj, proj, proj, proj,
      conv_w[:, :GW], conv_w[:, GW:GW + ns2], conv_w[:, GW + ns2:],
      conv_b[:GW].reshape(1, GW), conv_b[GW:GW + ns2].reshape(1, ns2), conv_b[GW + ns2:].reshape(1, ns2),
      pad_row(dt_bias), pad_row(a_log), jnp.repeat(d_skip, SSM_HEAD_DIM).reshape(1, GW), norm_w.reshape(1, GW))


HG_CHUNK = 16


def _hgrn2_body(q_ref, f_ref, i_ref, g_ref, lb_ref, nw_ref, o_ref, st_ref):
    tl, ck = TIME_BLOCK, HG_CHUNK
    nh = GW // HG_HEAD

    @pl.when(pl.program_id(1) == 0)
    def _():
        st_ref[...] = jnp.zeros_like(st_ref)

    lb = lb_ref[...]
    row8 = _iota2((ck // 2, GW), 0)
    tri16 = (_iota2((ck, ck), 0) >= _iota2((ck, ck), 1)).astype(BF16)

    def chunk(c, carry):
        r0 = pl.multiple_of(c * ck, ck)
        q = _silu(q_ref[pl.ds(r0, ck), :])
        f = lb + (1.0 - lb) * _sigmoid(f_ref[pl.ds(r0, ck), :])
        k = 1.0 - f
        v = i_ref[pl.ds(r0, ck), :]
        b = _dot_exact_lhs(tri16, jnp.log(f))
        bend = b[ck - 1:ck, :]
        qe = q * jnp.exp(b)
        kd = k * jnp.exp(bend - b)
        eend = jnp.exp(bend)
        half = ck // 2
        qh, bh = (q[:half], q[half:]), (b[:half], b[half:])
        acc = [[jnp.zeros((half, HG_HEAD), F32) for _ in range(nh)] for _ in range(2)]
        for s in range(ck):
            for p in range(s // half, 2):
                rel = bh[p] - b[s:s + 1, :]
                if p == s // half:
                    rel = jnp.where(row8 >= s - p * half, rel, -jnp.inf)
                w = qh[p] * k[s:s + 1, :] * jnp.exp(rel)
                for h in range(nh):
                    hs = slice(h * HG_HEAD, (h + 1) * HG_HEAD)
                    acc[p][h] = acc[p][h] + jnp.sum(w[:, hs], axis=-1, keepdims=True) * v[s:s + 1, hs]
        outs = []
        for h in range(nh):
            hs = slice(h * HG_HEAD, (h + 1) * HG_HEAD)
            st = st_ref[h]
            o = jnp.concatenate([acc[0][h], acc[1][h]], axis=0) + _dotb(qe[:, hs], st, NT)
            st_ref[h] = st * eend[:, hs] + _dotb(v[:, hs], kd[:, hs], TN)
            outs.append(o * lax.rsqrt(jnp.mean(o * o, axis=-1, keepdims=True) + NORM_EPS))
        o = jnp.concatenate(outs, axis=1) * nw_ref[...] * _silu(g_ref[pl.ds(r0, ck), :])
        o_ref[pl.ds(r0, ck), :] = o.astype(BF16)
        return carry

    lax.fori_loop(0, tl // ck, chunk, 0, unroll=8)


def _hgrn2(proj, bsz, seq, lower_bound, norm_w):
    nl = seq // TIME_BLOCK
    return pl.pallas_call(
        _hgrn2_body,
        grid=(bsz, nl),
        in_specs=[
            _col_spec(GW, C_HG_Q, nl), _col_spec(GW, C_HG_F, nl), _col_spec(GW, C_HG_I, nl), _col_spec(GW, C_HG_G, nl),
            _row_spec(GW), _row_spec(GW),
        ],
        out_specs=pl.BlockSpec((TIME_BLOCK, GW), lambda b, l: (b * nl + l, 0)),
        out_shape=jax.ShapeDtypeStruct((bsz * seq, GW), BF16),
        scratch_shapes=[pltpu.VMEM((GW // HG_HEAD, HG_HEAD, HG_HEAD), F32)],
        compiler_params=_cparams(("arbitrary", "arbitrary")),
        name="hgrn2",
    )(proj, proj, proj, proj, lower_bound.reshape(1, GW), norm_w.reshape(1, GW))


RW_T = 64
RW_HP = 2
RW_GL = RW_HP * RW_HEAD_DIM
RW_NG = RW_HEADS // RW_HP
RW_GB = 4


def _rwkv7_body(r_ref, k_ref, v_ref, wd_ref, ad_ref, gd_ref, mur_ref, muk_ref, muv_ref, muwd_ref, muad_ref, mugd_ref,
                w0_ref, w2_ref, a0_ref, a2_ref, g2_ref, kkw_ref, kaw_ref, rkw_ref, lnw_ref, lnb_ref,
                o_ref, car_r, car_k, car_v, car_wd, car_ad, car_gd, st_ref,
                kap_s, bet_s, kh_s, rho_s, v_s, c_s, y_s,
                r1_u, r2_u, mm_u, cc_u):
    tl, ct = TIME_BLOCK, RW_T
    srows = RW_HP * ct

    @pl.when(pl.program_id(1) == 0)
    def _():
        for car in (car_r, car_k, car_v, car_wd, car_ad, car_gd):
            car[...] = jnp.zeros_like(car)
        st_ref[...] = jnp.zeros_like(st_ref)

    def tshift(x_ref, car, mu_ref):
        x = x_ref[...]
        row = _iota2(x.shape, 0)
        prev = jnp.where(row == 0, car[0:1, :], pltpu.roll(x, 1, 0))
        car[0:1, :] = x[tl - 1:tl, :]
        return x + (prev - x) * mu_ref[...]

    r = tshift(r_ref, car_r, mur_ref)
    k = tshift(k_ref, car_k, muk_ref)
    v = tshift(v_ref, car_v, muv_ref)
    wd = tshift(wd_ref, car_wd, muwd_ref)
    ad = tshift(ad_ref, car_ad, muad_ref)
    gd = tshift(gd_ref, car_gd, mugd_ref)

    lw = -jnp.exp(-_softplus(-(w0_ref[...] + _dotb(jnp.tanh(wd), w2_ref[...]))) - 0.5)
    a = _sigmoid(a0_ref[...] + _dotb(ad, a2_ref[...]))
    g = _dotb(_sigmoid(gd), g2_ref[...])
    seg = (_iota2((GW, GW), 0) // RW_HEAD_DIM == _iota2((GW, GW), 1) // RW_HEAD_DIM).astype(BF16)
    kkr = k * kkw_ref[...]
    kk = kkr / jnp.maximum(jnp.sqrt(_dot_exact_rhs(kkr * kkr, seg)), 1e-12)
    k2 = k * (1.0 + (a - 1.0) * kaw_ref[...])
    blocktri = ((_iota2((tl, tl), 0) >= _iota2((tl, tl), 1))
                & (_iota2((tl, tl), 0) // ct == _iota2((tl, tl), 1) // ct)).astype(BF16)
    c = _dot_exact_lhs(blocktri, lw)
    enc = jnp.exp(-c)
    kap_s[...] = kk * jnp.exp(c - lw)
    bet_s[...] = kk * a * enc
    kh_s[...] = k2 * enc
    rho_s[...] = r * jnp.exp(c)
    v_s[...] = v
    c_s[...] = c

    rs = _iota2((srows, RW_GL), 0)
    hmask = (rs // ct) == (_iota2((srows, RW_GL), 1) // RW_HEAD_DIM)
    ri, cj = _iota2((srows, srows), 0), _iota2((srows, srows), 1)
    same = (ri // ct) == (cj // ct)
    strict = same & (ri > cj)
    incl = same & (ri >= cj)
    eye = (ri == cj).astype(F32)

    def stack(x):
        return jnp.where(hmask, jnp.concatenate([x] * RW_HP, axis=0), 0.0)

    nck = tl // ct
    bnt = (((2,), (2,)), ((0,), (0,)))
    bnn = (((2,), (1,)), ((0,), (0,)))
    btn = (((1,), (1,)), ((0,), (0,)))

    def bdot(a_, b_, dims):
        return lax.dot_general(a_.astype(BF16), b_.astype(BF16), dims, preferred_element_type=F32)

    def stack3(x):
        return jnp.where(hmask[None], jnp.concatenate([x] * RW_HP, axis=1), 0.0).astype(BF16)

    for g0 in range(0, RW_NG, RW_GB):
        to3 = lambda ref: jnp.concatenate(
            [ref[:, gi * RW_GL:(gi + 1) * RW_GL].reshape(nck, ct, RW_GL) for gi in range(g0, g0 + RW_GB)], axis=0)
        eup = jnp.exp(to3(c_s)[:, ct - 1:ct, :])
        bet, kh = to3(bet_s), to3(kh_s)
        kaps, bets, khs = stack3(to3(kap_s)), stack3(bet), stack3(kh)
        rhos, vs = stack3(to3(rho_s)), stack3(to3(v_s))
        bet_e, kh_e = stack3(bet * eup), stack3(kh * eup)
        bk = jnp.concatenate([bets, khs], axis=1)
        kap_bk = bdot(kaps, bk, bnt)
        rho_bk = bdot(rhos, bk, bnt)
        a_ab = jnp.where(strict[None], kap_bk[:, :, :srows], 0.0)
        a_ak = jnp.where(strict[None], kap_bk[:, :, srows:], 0.0)
        a_rb = jnp.where(incl[None], rho_bk[:, :, :srows], 0.0)
        a_rk = jnp.where(incl[None], rho_bk[:, :, srows:], 0.0)
        pw = -a_ab
        inv_g = eye[None] + pw
        for _ in range(int(math.log2(ct)) - 1):
            pw = bdot(pw, pw, bnn)
            inv_g = inv_g + bdot(inv_g, pw, bnn)
        us = slice(g0 * nck, (g0 + RW_GB) * nck)
        p1 = bdot(inv_g, kaps, bnn)
        p2 = bdot(inv_g, bdot(a_ak, vs, bnn), bnn)
        r1_u[us] = (rhos.astype(F32) - bdot(a_rb, p1, bnn)).astype(BF16)
        r2_u[us] = bdot(a_rk, vs, bnn) - bdot(a_rb, p2, bnn)
        mm_u[us] = bdot(p1, bet_e, btn).astype(BF16)
        cc_u[us] = bdot(vs, kh_e, btn) - bdot(p2, bet_e, btn)

    def chunk(ci, carry):
        r0 = pl.multiple_of(ci * ct, ct)
        for gi in range(RW_NG):
            un = gi * nck + ci
            gl = slice(gi * RW_GL, (gi + 1) * RW_GL)
            tail = c_s[pl.ds(pl.multiple_of(r0 + ct - SUBLANES, SUBLANES), SUBLANES), gl]
            eup = jnp.exp(tail[SUBLANES - 1:SUBLANES, :])
            s0 = st_ref[gi]
            s16 = s0.astype(BF16)
            ys = _dotb(r1_u[un], s16, NT) + r2_u[un]
            y = ys[0:ct]
            for hh in range(1, RW_HP):
                y = y + ys[hh * ct:(hh + 1) * ct]
            y_s[pl.ds(r0, ct), gl] = y
            st_ref[gi] = s0 * eup - _dotb(s16, mm_u[un]) + cc_u[un]
        return carry

    lax.fori_loop(0, tl // ct, chunk, 0)

    y = y_s[...]
    inv_n = 1.0 / RW_HEAD_DIM
    mean = _dot_exact_rhs(y, seg) * inv_n
    yc = y - mean
    var = _dot_exact_rhs(yc * yc, seg) * inv_n
    yn = yc * lax.rsqrt(var + RW_GN_EPS) * lnw_ref[...] + lnb_ref[...]
    bonus = _dot_exact_rhs(r * k2 * rkw_ref[...], seg) * v
    o_ref[...] = ((yn + bonus) * g).astype(BF16)


def _rwkv7(proj, bsz, seq, mu, w0, w2, a0, a2, g2, k_k, k_a, r_k, ln_w, ln_b):
    nl = seq // TIME_BLOCK
    row = lambda v_: v_.reshape(1, -1)
    padl = lambda v_: jnp.pad(v_, (0, LANES - v_.shape[0])).reshape(1, LANES)
    padr = lambda m: jnp.pad(m, ((0, LANES - m.shape[0]), (0, 0))).astype(BF16)
    o = 3 * GW
    mus = [row(mu[:GW]), row(mu[GW:2 * GW]), row(mu[2 * GW:o]), padl(mu[o:o + RW_LORA]),
           padl(mu[o + RW_LORA:o + 2 * RW_LORA]), row(mu[o + 2 * RW_LORA:])]
    f32buf = lambda w: pltpu.VMEM((TIME_BLOCK, w), F32)
    car = lambda w: pltpu.VMEM((SUBLANES, w), F32)
    units = (TIME_BLOCK // RW_T) * RW_NG
    srows = RW_HP * RW_T
    return pl.pallas_call(
        _rwkv7_body,
        grid=(bsz, nl),
        in_specs=[
            _col_spec(GW, C_RW_R, nl), _col_spec(GW, C_RW_K, nl), _col_spec(GW, C_RW_V, nl),
            _col_spec(LANES, C_RW_WD, nl), _col_spec(LANES, C_RW_AD, nl), _col_spec(RW_GATE, C_RW_GD, nl),
            _row_spec(GW), _row_spec(GW), _row_spec(GW), _row_spec(LANES), _row_spec(LANES), _row_spec(RW_GATE),
            _row_spec(GW), _full_spec((LANES, GW)), _row_spec(GW), _full_spec((LANES, GW)), _full_spec((RW_GATE, GW)),
            _row_spec(GW), _row_spec(GW), _row_spec(GW), _row_spec(GW), _row_spec(GW),
        ],
        out_specs=pl.BlockSpec((TIME_BLOCK, GW), lambda b, l: (b * nl + l, 0)),
        out_shape=jax.ShapeDtypeStruct((bsz * seq, GW), BF16),
        scratch_shapes=[car(GW), car(GW), car(GW), car(LANES), car(LANES), car(RW_GATE),
                        pltpu.VMEM((RW_NG, RW_GL, RW_GL), F32)] + [f32buf(GW)] * 7 + [
                            pltpu.VMEM((units, srows, RW_GL), BF16), pltpu.VMEM((units, srows, RW_GL), F32),
                            pltpu.VMEM((units, RW_GL, RW_GL), BF16), pltpu.VMEM((units, RW_GL, RW_GL), F32)],
        compiler_params=_cparams(("arbitrary", "arbitrary")),
        name="rwkv7",
    )(proj, proj, proj, proj, proj, proj, *mus,
      row(w0), padr(w2), row(a0), padr(a2), g2.astype(BF16), row(k_k), row(k_a), row(r_k), row(ln_w), row(ln_b))


ROUTE_TM = 256
R_E1, R_E2, R_G1, R_G2, R_K1, R_K2 = 0, 1, 2, 3, 4, 5


def _outproj_router_body(h_ref, ya_ref, yb_ref, yc_ref, yd_ref, wo_ref, nw_ref, wr_ref, br_ref,
                         ho_ref, xn_ref, route_ref, cnt_ref, cnt_s):
    tm = ROUTE_TM

    @pl.when(pl.program_id(0) == 0)
    def _():
        cnt_s[...] = jnp.zeros_like(cnt_s)

    acc = h_ref[...]
    for gi, y_ref in enumerate((ya_ref, yb_ref, yc_ref, yd_ref)):
        acc = acc + jnp.dot(y_ref[...], wo_ref[gi * GW:(gi + 1) * GW, :], preferred_element_type=F32)
    ho_ref[...] = acc
    xn = acc * lax.rsqrt(jnp.mean(acc * acc, axis=-1, keepdims=True) + NORM_EPS) * nw_ref[...]
    xn_ref[...] = xn
    logits = _dot3(xn, wr_ref[...]) + br_ref[...]

    lane = _iota2((tm, LANES), 1)
    neg = -jnp.inf
    gl = jnp.where(lane < N_GROUPS, logits, neg)
    mg = jnp.max(gl, axis=-1, keepdims=True)
    g_top = 1.0 / jnp.sum(jnp.exp(gl - mg), axis=-1, keepdims=True)
    g_idx = jnp.min(jnp.where(gl == mg, lane, LANES), axis=-1, keepdims=True)
    lo = N_GROUPS + E_PER_GROUP * g_idx
    el = jnp.where(lane >= lo, jnp.where(lane < lo + E_PER_GROUP, logits, neg), neg)
    me = jnp.max(el, axis=-1, keepdims=True)
    se = jnp.sum(jnp.exp(el - me), axis=-1, keepdims=True)
    i1 = jnp.min(jnp.where(el == me, lane, LANES), axis=-1, keepdims=True)
    el2 = jnp.where(lane == i1, neg, el)
    m2 = jnp.max(el2, axis=-1, keepdims=True)
    i2 = jnp.min(jnp.where(el2 == m2, lane, LANES), axis=-1, keepdims=True)
    p1 = 1.0 / se
    p2 = jnp.exp(m2 - me) / se
    gate1 = g_top * p1 / (p1 + p2)
    gate2 = g_top * p2 / (p1 + p2)
    e1 = i1 - N_GROUPS
    e2 = i2 - N_GROUPS

    oh1 = lane == e1
    oh2 = lane == e2
    oh = jnp.where(oh1, 1.0, jnp.where(oh2, 1.0, 0.0))
    stri = (_iota2((tm, tm), 0) > _iota2((tm, tm), 1)).astype(BF16)
    before = jnp.dot(stri, oh.astype(BF16), preferred_element_type=F32) + cnt_s[0:1, :]
    k1 = jnp.sum(jnp.where(oh1, before, 0.0), axis=-1, keepdims=True)
    k2 = jnp.sum(jnp.where(oh2, before, 0.0), axis=-1, keepdims=True)
    cnt = cnt_s[0:1, :] + jnp.sum(oh, axis=0, keepdims=True)
    cnt_s[0:1, :] = cnt
    cnt_ref[...] = jnp.broadcast_to(cnt, cnt_ref.shape)

    rec = jnp.where(lane == R_E1, e1.astype(F32), 0.0)
    rec = jnp.where(lane == R_E2, e2.astype(F32), rec)
    rec = jnp.where(lane == R_G1, gate1, rec)
    rec = jnp.where(lane == R_G2, gate2, rec)
    rec = jnp.where(lane == R_K1, k1, rec)
    rec = jnp.where(lane == R_K2, k2, rec)
    route_ref[...] = rec


def _outproj_router(h, ys, w_out16, norm_w, wr, br):
    n = h.shape[0]
    tm = ROUTE_TM
    rows = lambda w: pl.BlockSpec((tm, w), lambda i: (i, 0))
    whole = lambda shape: pl.BlockSpec(shape, lambda i: (0, 0))
    return pl.pallas_call(
        _outproj_router_body,
        grid=(n // tm,),
        in_specs=[rows(D_MODEL), rows(GW), rows(GW), rows(GW), rows(GW), whole((D_MODEL, D_MODEL)),
                  whole((1, D_MODEL)), whole((D_MODEL, LANES)), whole((1, LANES))],
        out_specs=[rows(D_MODEL), rows(D_MODEL), rows(LANES), whole((SUBLANES, LANES))],
        out_shape=[jax.ShapeDtypeStruct((n, D_MODEL), F32), jax.ShapeDtypeStruct((n, D_MODEL), F32),
                   jax.ShapeDtypeStruct((n, LANES), F32), jax.ShapeDtypeStruct((SUBLANES, LANES), F32)],
        scratch_shapes=[pltpu.VMEM((SUBLANES, LANES), F32)],
        compiler_params=_cparams(("arbitrary",)),
        name="outproj_router",
    )(h, *ys, w_out16, norm_w.reshape(1, D_MODEL), wr, br)


MOE_BLK = 256


def _moe_n_blocks(n_assign):
    return (n_assign + N_EXPERTS * (MOE_BLK - 1) + MOE_BLK - 1) // MOE_BLK


def _ffn_body(bexp_ref, stok_ref, sdst_ref, nused_ref, xn_hbm, wg_ref, wu_ref, wd_ref, y_hbm,
              xbuf, obuf, gsem, ssem):
    b = pl.program_id(0)
    last_blk = pl.num_programs(0) - 2
    nused = nused_ref[0]
    blk = MOE_BLK
    dump0 = y_hbm.shape[0] - 2 * blk
    slot = b % 2
    other = 1 - slot

    def gather_row(j, r, s, prio=0):
        t = stok_ref[j * blk + r]
        pltpu.make_async_copy(xn_hbm.at[pl.ds(t, 1)], xbuf.at[s, pl.ds(r, 1)], gsem.at[s]).start(priority=prio)

    def scatter_row(j, r, s, all_dump, prio=0):
        a = sdst_ref[j * blk + r]
        dst = jnp.where(jnp.logical_or(a < 0, all_dump), dump0 + s * blk + r, a)
        pltpu.make_async_copy(obuf.at[s, pl.ds(r, 1)], y_hbm.at[pl.ds(dst, 1)], ssem.at[s]).start(priority=prio)

    def gather_wait(s):
        pltpu.make_async_copy(xn_hbm.at[pl.ds(0, blk)], xbuf.at[s], gsem.at[s]).wait()

    def scatter_wait(s):
        pltpu.make_async_copy(obuf.at[s], y_hbm.at[pl.ds(0, blk)], ssem.at[s]).wait()

    @pl.when(b == 0)
    def _():
        obuf[...] = jnp.zeros_like(obuf)
        for s in range(2):
            pltpu.make_async_copy(obuf.at[s], y_hbm.at[pl.ds(dump0 + s * blk, blk)], ssem.at[s]).start()
        for s in range(2):
            pltpu.make_async_copy(obuf.at[s], y_hbm.at[pl.ds(dump0 + s * blk, blk)], ssem.at[s]).wait()

        def row(r, c):
            gather_row(0, r, 0)
            return c
        lax.fori_loop(0, blk, row, 0, unroll=8)

    @pl.when(b < nused)
    def _():
        gather_wait(slot)

        @pl.when(b >= 1)
        def _():
            scatter_wait(slot)

        nxt = jnp.minimum(b + 1, last_blk)
        prv = jnp.maximum(b - 1, 0)
        for r in range(blk):
            gather_row(nxt, r, other, r % 2)
        for r in range(blk):
            scatter_row(prv, r, other, b == 0, r % 2)
        x = xbuf[slot].astype(BF16)
        hid = _silu(jnp.dot(x, wg_ref[0], preferred_element_type=F32)) * jnp.dot(x, wu_ref[0], preferred_element_type=F32)
        obuf[slot] = jnp.dot(hid.astype(BF16), wd_ref[0].astype(BF16), preferred_element_type=F32)

    @pl.when(b == nused)
    def _():
        gather_wait(slot)
        scatter_wait(slot)

        def row(r, c):
            scatter_row(b - 1, r, other, False)
            return c
        lax.fori_loop(0, blk, row, 0, unroll=8)
        scatter_wait(other)


def _moe_ffn(xn, wg16, wu16, wd_all, layer, bexp, stok, sdst, nused):
    n = xn.shape[0]
    n_steps = bexp.shape[0]
    wspec = lambda shape: pl.BlockSpec((1,) + shape, lambda b, be, st, sd, nu: (be[b], 0, 0))
    wd_spec = pl.BlockSpec((None, 1, D_EXPERT, D_MODEL), lambda b, be, st, sd, nu: (layer, be[b], 0, 0))
    return pl.pallas_call(
        _ffn_body,
        grid_spec=pltpu.PrefetchScalarGridSpec(
            num_scalar_prefetch=4,
            grid=(n_steps,),
            in_specs=[pl.BlockSpec(memory_space=pl.ANY), wspec((D_MODEL, D_EXPERT)), wspec((D_MODEL, D_EXPERT)),
                      wd_spec],
            out_specs=pl.BlockSpec(memory_space=pl.ANY),
            scratch_shapes=[pltpu.VMEM((2, MOE_BLK, D_MODEL), F32), pltpu.VMEM((2, MOE_BLK, D_MODEL), F32),
                            pltpu.SemaphoreType.DMA((2,)), pltpu.SemaphoreType.DMA((2,))],
        ),
        out_shape=jax.ShapeDtypeStruct((2 * n + 2 * MOE_BLK, D_MODEL), F32),
        compiler_params=_cparams(("arbitrary",)),
        name="moe_ffn",
    )(bexp, stok, sdst, nused, xn, wg16, wu16, wd_all)


def _cast_body(x_ref, o_ref):
    o_ref[...] = x_ref[...].astype(BF16)


def _expert_weights_bf16(w, layer):
    _, e, a, b_ = w.shape
    return pl.pallas_call(
        _cast_body, grid=(e,),
        in_specs=[pl.BlockSpec((None, 1, a, b_), lambda i: (layer, i, 0, 0))],
        out_specs=pl.BlockSpec((1, a, b_), lambda i: (i, 0, 0)),
        out_shape=jax.ShapeDtypeStruct((e, a, b_), BF16),
        compiler_params=_cparams(("arbitrary",)), name="cast_bf16",
    )(w)


def _dispatch_plan(route, counts, n_blocks):
    n = route.shape[0]
    eid = route[:, R_E1:R_E2 + 1].astype(jnp.int32)
    rank = route[:, R_K1:R_K2 + 1].astype(jnp.int32)
    cnt = counts[0, :N_EXPERTS].astype(jnp.int32)
    padded = (cnt + MOE_BLK - 1) // MOE_BLK * MOE_BLK
    pad_end = jnp.cumsum(padded)
    pad_start = pad_end - padded
    onehot = eid[:, :, None] == jnp.arange(N_EXPERTS, dtype=jnp.int32)
    slot = (jnp.sum(jnp.where(onehot, pad_start, 0), axis=-1) + rank).reshape(-1)
    n_slots = n_blocks * MOE_BLK
    asg = jnp.full((n_slots,), -1, jnp.int32).at[slot].set(jnp.arange(2 * n, dtype=jnp.int32))
    stok = jnp.maximum(asg, 0) // 2
    sdst = jnp.where(asg >= 0, (asg % 2) * n + stok, -1)
    starts = jnp.arange(n_blocks + 1, dtype=jnp.int32) * MOE_BLK
    bexp = jnp.minimum(jnp.sum(pad_end[None, :] <= starts[:, None], axis=1), N_EXPERTS - 1)
    nused = (pad_end[-1] // MOE_BLK).reshape(1).astype(jnp.int32)
    return bexp.astype(jnp.int32), stok, sdst, nused


def _combine_body(h_ref, y1_ref, y2_ref, route_ref, nw_ref, o_ref, *, final):
    rt = route_ref[...]
    out = h_ref[...] + rt[:, R_G1:R_G1 + 1] * y1_ref[...] + rt[:, R_G2:R_G2 + 1] * y2_ref[...]
    if final:
        out = out * lax.rsqrt(jnp.mean(out * out, axis=-1, keepdims=True) + NORM_EPS) * nw_ref[...]
    o_ref[...] = out


def _combine(h, y2, route, norm_w, final, tm=256):
    n = h.shape[0]
    return pl.pallas_call(
        functools.partial(_combine_body, final=final),
        grid=(n // tm,),
        in_specs=[pl.BlockSpec((tm, D_MODEL), lambda i: (i, 0)), pl.BlockSpec((tm, D_MODEL), lambda i: (i, 0)),
                  pl.BlockSpec((tm, D_MODEL), lambda i: (i + n // tm, 0)),
                  pl.BlockSpec((tm, LANES), lambda i: (i, 0)), pl.BlockSpec((1, D_MODEL), lambda i: (0, 0))],
        out_specs=pl.BlockSpec((tm, D_MODEL), lambda i: (i, 0)),
        out_shape=jax.ShapeDtypeStruct((n, D_MODEL), F32),
        compiler_params=_cparams(("arbitrary",)),
        name="combine",
    )(h, y2, y2, route, norm_w.reshape(1, D_MODEL))


def kernel(x, w_in, w_out, norm_mix_w, norm_ffn_w, final_norm_w, hg_lb_param, hg_norm_w, ssm_conv_w, ssm_conv_b, ssm_dt_bias, ssm_a_log, ssm_d, ssm_norm_w, rg_conv_w, rg_conv_b, rg_w_a, rg_b_a, rg_w_x, rg_b_x, rg_lambda, rw_mu, rw_w0, rw_w2, rw_a0, rw_a2, rw_g2, rw_k_k, rw_k_a, rw_r_k, rw_ln_w, rw_ln_b, router_group_w, router_group_b, router_expert_w, router_expert_b, moe_w_gate, moe_w_up, moe_w_down):
    bsz, seq, d = x.shape
    n = bsz * seq
    n_blocks = _moe_n_blocks(2 * n)
    sm = jax.nn.softmax(hg_lb_param.astype(F32), axis=0)
    lower_bounds = jnp.cumsum(sm, axis=0) - sm[0]
    h = x.reshape(n, d)
    for l in range(w_in.shape[0]):
        proj = _inproj(h, norm_mix_w[l], _relayout_w_in(w_in[l]))
        ya = _hgrn2(proj, bsz, seq, lower_bounds[l], hg_norm_w[l])
        yb = _ssd(proj, bsz, seq, ssm_conv_w[l], ssm_conv_b[l], ssm_dt_bias[l], ssm_a_log[l], ssm_d[l], ssm_norm_w[l])
        yc = _rglru(proj, bsz, seq, rg_conv_w[l], rg_conv_b[l], rg_w_a[l], rg_b_a[l], rg_w_x[l], rg_b_x[l], rg_lambda[l])
        yd = _rwkv7(proj, bsz, seq, rw_mu[l], rw_w0[l], rw_w2[l], rw_a0[l], rw_a2[l], rw_g2[l], rw_k_k[l], rw_k_a[l],
                    rw_r_k[l], rw_ln_w[l], rw_ln_b[l])
        wr = jnp.pad(jnp.concatenate([router_group_w[l], router_expert_w[l]], axis=1),
                     ((0, 0), (0, LANES - N_GROUPS - N_EXPERTS)))
        br = jnp.pad(jnp.concatenate([router_group_b[l], router_expert_b[l]]), (0, LANES - N_GROUPS - N_EXPERTS)).reshape(1, LANES)
        h, xn, route, counts = _outproj_router(h, (ya, yb, yc, yd), w_out[l].astype(BF16), norm_ffn_w[l], wr, br)
        bexp, stok, sdst, nused = _dispatch_plan(route, counts, n_blocks)
        y2 = _moe_ffn(xn, _expert_weights_bf16(moe_w_gate, l), _expert_weights_bf16(moe_w_up, l),
                      moe_w_down, l, bexp, stok, sdst, nused)
        h = _combine(h, y2, route, final_norm_w, final=(l == w_in.shape[0] - 1))
    return h.reshape(bsz, seq, d)
```
